```python
import jax, jax.numpy as jnp
from jax import lax
import numpy as np

D_MODEL = 1024
BATCH = 4
SEQ = 8192
DEPTH = 2

GRID_W = 64
CTX_LEN = 256
HEAD_DIM = 64
D_A = D_MODEL // 2
D_B = D_MODEL // 4
D_C = D_MODEL // 4
D_MIX = D_A + D_B + D_C
H_A = D_A // HEAD_DIM
H_B = D_B // HEAD_DIM
H_C = D_C // HEAD_DIM
D_IN = 2 * D_A + 2 * D_B + D_C
CHUNK = 128
CONV_W = 4
CONV_LEFT = 2
RG_C = 8.0
D_FF = -(-(8 * D_MODEL) // (3 * 256)) * 256
N_MOD = 6
DEEPNORM_ALPHA = (2 * DEPTH) ** 0.25
DEEPNORM_BETA = (8 * DEPTH) ** -0.25
LN_EPS = 1e-6
POS_BASE = 10000.0

kernel_name = "hybrid_rglru_chunkmlp_fourier_dit"


def layer_norm(x, g=None, b=None):
    xf = x.astype(jnp.float32)
    mu = jnp.mean(xf, -1, keepdims=True)
    var = jnp.mean(jnp.square(xf - mu), -1, keepdims=True)
    y = (xf - mu) * lax.rsqrt(var + LN_EPS)
    if g is not None:
        y = y * g.astype(jnp.float32) + b.astype(jnp.float32)
    return y.astype(x.dtype)


def rms_norm(x, g):
    xf = x.astype(jnp.float32)
    y = xf * lax.rsqrt(jnp.mean(jnp.square(xf), -1, keepdims=True) + LN_EPS)
    return (y * g.astype(jnp.float32)).astype(x.dtype)


def pos_embed_2d(rows, dim):
    quarter = dim // 4
    freqs = POS_BASE ** (-jnp.arange(quarter, dtype=jnp.float32) / quarter)
    r = jnp.repeat(jnp.arange(rows, dtype=jnp.float32), GRID_W)
    col = jnp.tile(jnp.arange(GRID_W, dtype=jnp.float32), rows)

    def enc(p):
        ang = p[:, None] * freqs[None, :]
        return jnp.concatenate([jnp.sin(ang), jnp.cos(ang)], -1)

    return jnp.concatenate([enc(r), enc(col)], -1)


def dwconv_centred(x, w, b):
    L = x.shape[1]
    xp = jnp.pad(x, ((0, 0), (CONV_LEFT, CONV_W - 1 - CONV_LEFT), (0, 0)))
    y = b
    for k in range(CONV_W):
        y = y + w[k] * xp[:, k:k + L]
    return y


def _combine(left, right):
    a_l, b_l = left
    a_r, b_r = right
    return a_l * a_r, a_r * b_l + b_r


def linear_scan(a, b, reverse):
    return lax.associative_scan(_combine, (a, b), reverse=reverse, axis=1)


def rglru_bidir(xa, conv_w, conv_b, wa, ba, wx, bx, lam, h0=None):
    Bsz, L, _ = xa.shape
    xc = dwconv_centred(xa, conv_w, conv_b).astype(jnp.float32)
    xh = xc.reshape(Bsz, L, H_A, HEAD_DIM)
    r = jax.nn.sigmoid(jnp.einsum("blhc,dhce->dblhe", xh, wa.astype(jnp.float32)).reshape(2, Bsz, L, D_A)
                       + ba.astype(jnp.float32)[:, None, None])
    i = jax.nn.sigmoid(jnp.einsum("blhc,dhce->dblhe", xh, wx.astype(jnp.float32)).reshape(2, Bsz, L, D_A)
                       + bx.astype(jnp.float32)[:, None, None])
    log_a = -RG_C * r * jax.nn.softplus(-lam.astype(jnp.float32))[:, None, None]
    a = jnp.exp(log_a)
    u = jnp.sqrt(-jnp.expm1(2.0 * log_a)) * (i * xc[None])
    acum_f, h_f = linear_scan(a[0], u[0], False)
    acum_b, h_b = linear_scan(a[1], u[1], True)
    if h0 is not None:
        h_f = h_f + acum_f * h0[0][:, None]
        h_b = h_b + acum_b * h0[1][:, None]
    return h_f, h_b


def spatial_gating(uv, ws, bs):
    Bsz, L, _ = uv.shape
    z = jax.nn.gelu(uv, approximate=False)
    u, v = z[..., :D_B], z[..., D_B:]
    v = layer_norm(v.reshape(Bsz, L // CHUNK, CHUNK, H_B, HEAD_DIM))
    s = jnp.einsum("hpq,bnqhc->bnphc", ws, v) + jnp.transpose(bs)[:, :, None]
    return u * s.reshape(Bsz, L, D_B)


def fourier_mix(xf, wf):
    Bsz, L, _ = xf.shape
    z = xf.reshape(Bsz, L, H_C, HEAD_DIM).astype(jnp.float32)
    f = jnp.fft.fft2(z, axes=(1, 3), norm="ortho").real
    return jnp.einsum("blhc,hce->blhe", f, wf.astype(jnp.float32)).reshape(Bsz, L, D_C).astype(xf.dtype)


def mix_outputs(p, h_lru, sg_ws, sg_b, fourier_w, g_mix, w_out):
    y_a = jax.nn.gelu(p[..., D_A:2 * D_A], approximate=False) * h_lru.astype(p.dtype)
    y_b = spatial_gating(p[..., 2 * D_A:2 * D_A + 2 * D_B], sg_ws, sg_b)
    y_c = fourier_mix(p[..., 2 * D_A + 2 * D_B:], fourier_w)
    y = jnp.concatenate([rms_norm(y_a, g_mix[:D_A]),
                         rms_norm(y_b, g_mix[D_A:D_A + D_B]),
                         rms_norm(y_c, g_mix[D_A + D_B:])], -1)
    return y @ w_out


def swiglu(h, w_up, w_down):
    gu = h @ w_up
    return (jax.nn.silu(gu[..., :D_FF]) * gu[..., D_FF:]) @ w_down


def setup_inputs(seed: int = 0) -> dict:
    key = jax.random.key(seed)
    ks = jax.random.split(key, 26)
    f32 = jnp.float32

    def nrm(k, shape, s):
        return jax.random.normal(k, shape, f32) * s

    a0 = jax.random.uniform(ks[13], (DEPTH, 2, D_A), f32, 0.9, 0.999)
    return {
        "x": nrm(ks[0], (BATCH, SEQ, D_MODEL), 1.0),
        "c": nrm(ks[1], (BATCH, D_MODEL), 1.0),
        "ctx": nrm(ks[2], (BATCH, CTX_LEN, D_MODEL), 1.0),
        "c_ctx": nrm(ks[3], (D_MODEL,), 1.0),
        "w_mod": nrm(ks[4], (DEPTH, D_MODEL, N_MOD * D_MODEL), 0.5 * D_MODEL ** -0.5),
        "b_mod": nrm(ks[5], (DEPTH, N_MOD * D_MODEL), 0.02),
        "w_in": nrm(ks[6], (DEPTH, D_MODEL, D_IN), D_MODEL ** -0.5),
        "conv_w": nrm(ks[7], (DEPTH, CONV_W, D_A), CONV_W ** -0.5),
        "conv_b": nrm(ks[8], (DEPTH, D_A), 0.02),
        "lru_wa": nrm(ks[9], (DEPTH, 2, H_A, HEAD_DIM, HEAD_DIM), HEAD_DIM ** -0.5),
        "lru_ba": nrm(ks[10], (DEPTH, 2, D_A), 0.02),
        "lru_wx": nrm(ks[11], (DEPTH, 2, H_A, HEAD_DIM, HEAD_DIM), HEAD_DIM ** -0.5),
        "lru_bx": nrm(ks[12], (DEPTH, 2, D_A), 0.02),
        "lru_lam": jnp.log(a0) - jnp.log1p(-a0),
        "sg_ws": nrm(ks[14], (DEPTH, H_B, CHUNK, CHUNK), CHUNK ** -0.5),
        "sg_b": 1.0 + nrm(ks[15], (DEPTH, H_B, CHUNK), 0.02),
        "fourier_w": nrm(ks[16], (DEPTH, H_C, HEAD_DIM, HEAD_DIM), HEAD_DIM ** -0.5),
        "g_mix": 1.0 + nrm(ks[17], (DEPTH, D_MIX), 0.02),
        "w_out": nrm(ks[18], (DEPTH, D_MIX, D_MODEL), DEEPNORM_BETA * D_MIX ** -0.5),
        "ln1_g": 1.0 + nrm(ks[19], (DEPTH, D_MODEL), 0.02),
        "ln1_b": nrm(ks[20], (DEPTH, D_MODEL), 0.02),
        "w_up": nrm(ks[21], (DEPTH, D_MODEL, 2 * D_FF), D_MODEL ** -0.5),
        "w_down": nrm(ks[22], (DEPTH, D_FF, D_MODEL), DEEPNORM_BETA * D_FF ** -0.5),
        "ln2_g": 1.0 + nrm(ks[23], (DEPTH, D_MODEL), 0.02),
        "ln2_b": nrm(ks[24], (DEPTH, D_MODEL), 0.02),
    }


def reference(x, c, ctx, c_ctx, w_mod, b_mod, w_in, conv_w, conv_b, lru_wa, lru_ba, lru_wx, lru_bx,
              lru_lam, sg_ws, sg_b, fourier_w, g_mix, w_out, ln1_g, ln1_b, w_up, w_down, ln2_g, ln2_b):
    Bsz, L, D = x.shape
    rows = L // GRID_W
    xl = x + pos_embed_2d(rows, D).astype(x.dtype)[None]
    xc = ctx
    sc = jax.nn.silu(c)
    sc_ctx = jax.nn.silu(c_ctx)
    for l in range(DEPTH):
        last = l == DEPTH - 1
        mod_l = (sc @ w_mod[l] + b_mod[l]).reshape(Bsz, N_MOD, 1, D)
        mod_c = (sc_ctx @ w_mod[l] + b_mod[l]).reshape(N_MOD, 1, 1, D)
        lru = (conv_w[l], conv_b[l], lru_wa[l], lru_ba[l], lru_wx[l], lru_bx[l], lru_lam[l])
        mix = (sg_ws[l], sg_b[l], fourier_w[l], g_mix[l], w_out[l])

        hc = layer_norm(xc) * (1.0 + mod_c[1]) + mod_c[0]
        hl = layer_norm(xl) * (1.0 + mod_l[:, 1]) + mod_l[:, 0]
        if last:
            hf_c, hb_c = rglru_bidir(hc @ w_in[l][:, :D_A], *lru)
        else:
            pc = hc @ w_in[l]
            hf_c, hb_c = rglru_bidir(pc[..., :D_A], *lru)
            yc = mix_outputs(pc, hf_c + hb_c, *mix)
        pl = hl @ w_in[l]
        hf_l, hb_l = rglru_bidir(pl[..., :D_A], *lru, h0=(hf_c[:, -1], hb_c[:, 0]))
        yl = mix_outputs(pl, hf_l + hb_l, *mix)
        xl = layer_norm(DEEPNORM_ALPHA * xl + mod_l[:, 2] * yl, ln1_g[l], ln1_b[l])

        hl = layer_norm(xl) * (1.0 + mod_l[:, 4]) + mod_l[:, 3]
        xl = layer_norm(DEEPNORM_ALPHA * xl + mod_l[:, 5] * swiglu(hl, w_up[l], w_down[l]), ln2_g[l], ln2_b[l])

        if not last:
            xc = layer_norm(DEEPNORM_ALPHA * xc + mod_c[2] * yc, ln1_g[l], ln1_b[l])
            hc = layer_norm(xc) * (1.0 + mod_c[4]) + mod_c[3]
            xc = layer_norm(DEEPNORM_ALPHA * xc + mod_c[5] * swiglu(hc, w_up[l], w_down[l]), ln2_g[l], ln2_b[l])
    return xl
```

```python
import functools
import math

import jax
import jax.numpy as jnp
import numpy as np
from jax import lax
from jax.experimental import pallas as pl
from jax.experimental.pallas import tpu as pltpu

HEAD_DIM = 64
CHUNK = 128
GRID_W = 64
RG_C = 8.0
LN_EPS = 1e-6
POS_BASE = 10000.0
N_MOD = 6
SUBLANES = 8
LANES = 128
FF_CHUNK = 256
DFT_N2 = 128
VMEM_LIMIT = 56 * 1024 * 1024

F32 = jnp.float32
BF16 = jnp.bfloat16


def _dot(a, b):
    return jnp.dot(a, b, preferred_element_type=F32)


def _ln(x):
    mu = jnp.mean(x, axis=-1, keepdims=True)
    d = x - mu
    var = jnp.mean(d * d, axis=-1, keepdims=True)
    return d * lax.rsqrt(var + LN_EPS)


def _gelu(x):
    return 0.5 * x * (1.0 + lax.erf(x * (1.0 / math.sqrt(2.0))))


def _cparams(*sem):
    return pltpu.CompilerParams(dimension_semantics=sem, vmem_limit_bytes=VMEM_LIMIT)


def _mod_kernel(c_ref, w_ref, b_ref, o_ref):
    c = c_ref[...]
    s = (c * jax.nn.sigmoid(c)).astype(BF16)
    o_ref[...] = _dot(s, w_ref[...].astype(BF16)) + b_ref[...]


def _mod_call(cc, w_mod, b_mod):
    depth, d, nm = w_mod.shape
    tn = 1536
    return pl.pallas_call(
        _mod_kernel,
        grid=(depth, nm // tn),
        in_specs=[
            pl.BlockSpec((SUBLANES, d), lambda l, j: (0, 0)),
            pl.BlockSpec((None, d, tn), lambda l, j: (l, 0, j)),
            pl.BlockSpec((None, 1, tn), lambda l, j: (l, 0, j)),
        ],
        out_specs=pl.BlockSpec((None, SUBLANES, tn), lambda l, j: (l, 0, j)),
        out_shape=jax.ShapeDtypeStruct((depth, SUBLANES, nm), F32),
        compiler_params=_cparams("parallel", "parallel"),
        name="mod_vectors",
    )(cc, w_mod, b_mod.reshape(depth, 1, nm))


def _fold_kernel(cc_ref, sc_ref, wf_ref, o_ref):
    wf = wf_ref[...]
    n = wf.shape[0]
    o_ref[:, :n] = jnp.dot(cc_ref[...], wf, preferred_element_type=F32, precision=lax.Precision.HIGHEST)
    o_ref[:, n:] = jnp.dot(sc_ref[...], wf, preferred_element_type=F32, precision=lax.Precision.HIGHEST)


def _fold_call(cc_bd, sc_bd, wf_bd):
    n = wf_bd.shape[0]
    return pl.pallas_call(
        _fold_kernel,
        out_shape=jax.ShapeDtypeStruct((n, 2 * n), F32),
        name="fourier_weight_fold",
    )(cc_bd, sc_bd, wf_bd)


def _inproj_kernel(*refs, add_pos, d_main):
    if add_pos:
        x_ref, pos_ref, mod_ref, w_ref, wcs_ref, xo_ref, pm_ref, pq_ref = refs
        x = x_ref[...] + pos_ref[...]
        xo_ref[...] = x
    else:
        x_ref, mod_ref, w_ref, wcs_ref, pm_ref, pq_ref = refs
        x = x_ref[...]
    m = mod_ref[...]
    h = _ln(x) * (1.0 + m[1:2, :]) + m[0:1, :]
    p = _dot(h.astype(BF16), w_ref[...])
    pm_ref[...] = p[:, :d_main]
    pq_ref[...] = _dot(p[:, d_main:].astype(BF16), wcs_ref[...])


def _inproj_call(x, pos, mod, w_in, wcs, tile):
    b, l, d = x.shape
    d_in = w_in.shape[1]
    d_c = wcs.shape[0]
    d_main = d_in - d_c
    add_pos = pos is not None
    tok = lambda bi, i: (bi, i, 0)
    in_specs = [pl.BlockSpec((None, tile, d), tok)]
    args = [x]
    if add_pos:
        in_specs.append(pl.BlockSpec((tile, d), lambda bi, i: (i, 0)))
        args.append(pos)
    in_specs += [
        pl.BlockSpec((None, N_MOD, d), lambda bi, i: (bi, 0, 0)),
        pl.BlockSpec((d, d_in), lambda bi, i: (0, 0)),
        pl.BlockSpec((d_c, 2 * d_c), lambda bi, i: (0, 0)),
    ]
    args += [mod, w_in, wcs]
    out_specs = [pl.BlockSpec((None, tile, d_main), tok), pl.BlockSpec((None, tile, 2 * d_c), tok)]
    out_shape = [jax.ShapeDtypeStruct((b, l, d_main), F32), jax.ShapeDtypeStruct((b, l, 2 * d_c), F32)]
    if add_pos:
        out_specs.insert(0, pl.BlockSpec((None, tile, d), tok))
        out_shape.insert(0, jax.ShapeDtypeStruct((b, l, d), F32))
    return pl.pallas_call(
        functools.partial(_inproj_kernel, add_pos=add_pos, d_main=d_main),
        grid=(b, l // tile),
        in_specs=in_specs,
        out_specs=out_specs,
        out_shape=out_shape,
        compiler_params=_cparams("parallel", "parallel"),
        name="in_proj",
    )(*args)


def _lru_kernel(x_ref, prev_ref, next_ref, cw_ref, cb_ref, wg_ref, ba_ref, bx_ref, lam_ref, h0_ref,
                h_ref, hfin_ref, ext_ref, a_ref, b_ref, carry_ref, *, reverse, tile):
    i = pl.program_id(1)
    n = pl.num_programs(1)
    li = (n - 1 - i) if reverse else i
    d_a = x_ref.shape[-1]
    groups = tile // SUBLANES

    @pl.when(i == 0)
    def _():
        carry_ref[...] = h0_ref[...]

    x = x_ref[...]
    ext_ref[0:SUBLANES, :] = jnp.where(li == 0, 0.0, prev_ref[...])
    ext_ref[SUBLANES:SUBLANES + tile, :] = x
    ext_ref[SUBLANES + tile:, :] = jnp.where(li == n - 1, 0.0, next_ref[...])
    cw = cw_ref[...]
    xc = (cb_ref[...]
          + cw[0:1, :] * ext_ref[SUBLANES - 2:SUBLANES - 2 + tile, :]
          + cw[1:2, :] * ext_ref[SUBLANES - 1:SUBLANES - 1 + tile, :]
          + cw[2:3, :] * x
          + cw[3:4, :] * ext_ref[SUBLANES + 1:SUBLANES + 1 + tile, :])

    xcb = xc.astype(BF16)
    pair = 2 * HEAD_DIM
    rs, is_ = [], []
    for j in range(d_a // pair):
        g = _dot(xcb[:, j * pair:(j + 1) * pair], wg_ref[j])
        rs.append(g[:, :pair])
        is_.append(g[:, pair:])
    r = jax.nn.sigmoid(jnp.concatenate(rs, axis=-1) + ba_ref[...])
    ig = jax.nn.sigmoid(jnp.concatenate(is_, axis=-1) + bx_ref[...])
    lam = lam_ref[...]
    neg = -lam
    softplus = jnp.maximum(neg, 0.0) + jnp.log1p(jnp.exp(-jnp.abs(neg)))
    log_a = r * (-RG_C * softplus)
    a = jnp.exp(log_a)
    u = jnp.sqrt(-jnp.tanh(log_a) * (1.0 + a * a)) * (ig * xc)

    a3 = a.reshape(groups, SUBLANES, d_a)
    b3 = u.reshape(groups, SUBLANES, d_a)
    row = lax.broadcasted_iota(jnp.int32, (groups, SUBLANES, d_a), 1)
    for s in (1, 2, 4):
        if reverse:
            shift, valid = SUBLANES - s, row < SUBLANES - s
        else:
            shift, valid = s, row >= s
        a_sh = pltpu.roll(a3, shift, 1)
        b_sh = pltpu.roll(b3, shift, 1)
        b3 = jnp.where(valid, a3 * b_sh + b3, b3)
        a3 = jnp.where(valid, a3 * a_sh, a3)
    a_ref[...] = a3
    b_ref[...] = b3

    edge = 0 if reverse else SUBLANES - 1

    def body(g, carry):
        gi = (groups - 1 - g) if reverse else g
        h = a_ref[gi] * carry + b_ref[gi]
        h_ref[pl.ds(pl.multiple_of(gi * SUBLANES, SUBLANES), SUBLANES), :] = h
        return jnp.broadcast_to(h[edge:edge + 1, :], (SUBLANES, d_a))

    carry = lax.fori_loop(0, groups, body, carry_ref[...], unroll=8)
    carry_ref[...] = carry
    hfin_ref[...] = carry


def _lru_call(p_main, conv_w, conv_b, wg, ba, bx, lam, h0, *, reverse, tile):
    b, l, _ = p_main.shape
    d_a = conv_w.shape[-1]
    n = l // tile
    per = tile // SUBLANES
    nblk = l // SUBLANES
    pos = (lambda i: n - 1 - i) if reverse else (lambda i: i)
    vec = lambda shape: pl.BlockSpec(shape, lambda bi, i: (0,) * len(shape))
    return pl.pallas_call(
        functools.partial(_lru_kernel, reverse=reverse, tile=tile),
        grid=(b, n),
        in_specs=[
            pl.BlockSpec((None, tile, d_a), lambda bi, i: (bi, pos(i), 0)),
            pl.BlockSpec((None, SUBLANES, d_a), lambda bi, i: (bi, jnp.maximum(pos(i) * per - 1, 0), 0)),
            pl.BlockSpec((None, SUBLANES, d_a), lambda bi, i: (bi, jnp.minimum((pos(i) + 1) * per, nblk - 1), 0)),
            vec((4, d_a)), vec((1, d_a)), vec(wg.shape), vec((1, d_a)), vec((1, d_a)), vec((1, d_a)),
            pl.BlockSpec((None, SUBLANES, d_a), lambda bi, i: (bi, 0, 0)),
        ],
        out_specs=[
            pl.BlockSpec((None, tile, d_a), lambda bi, i: (bi, pos(i), 0)),
            pl.BlockSpec((None, SUBLANES, d_a), lambda bi, i: (bi, 0, 0)),
        ],
        out_shape=[jax.ShapeDtypeStruct((b, l, d_a), F32), jax.ShapeDtypeStruct((b, SUBLANES, d_a), F32)],
        scratch_shapes=[
            pltpu.VMEM((tile + 2 * SUBLANES, d_a), F32),
            pltpu.VMEM((tile // SUBLANES, SUBLANES, d_a), F32),
            pltpu.VMEM((tile // SUBLANES, SUBLANES, d_a), F32),
            pltpu.VMEM((SUBLANES, d_a), F32),
        ],
        compiler_params=_cparams("arbitrary", "arbitrary"),
        name="lru_scan_bwd" if reverse else "lru_scan_fwd",
    )(p_main, p_main, p_main, conv_w, conv_b, wg, ba, bx, lam, h0)


def _dft_direct_kernel(pq_ref, m_ref, y_ref):
    pq = pq_ref[...]
    dc = pq.shape[-1] // 2
    rhs = jnp.concatenate([pq[:, :dc], pq[:, dc:]], axis=0).astype(BF16)
    y_ref[...] = _dot(m_ref[...], rhs)


def _dft_direct_call(pq, m):
    b, l, dc2 = pq.shape
    return pl.pallas_call(
        _dft_direct_kernel,
        grid=(b,),
        in_specs=[pl.BlockSpec((None, l, dc2), lambda bi: (bi, 0, 0)),
                  pl.BlockSpec((l, 2 * l), lambda bi: (0, 0))],
        out_specs=pl.BlockSpec((None, l, dc2 // 2), lambda bi: (bi, 0, 0)),
        out_shape=jax.ShapeDtypeStruct((b, l, dc2 // 2), F32),
        compiler_params=_cparams("parallel"),
        name="dft_direct",
    )(pq, m)


def _dft_stage1_kernel(x_ref, m_ref, ct_ref, st_ref, br_ref, bi_ref, *, dc):
    n1 = x_ref.shape[0]
    m = m_ref[...]
    ncol = x_ref.shape[1] // (2 * dc)
    for j in range(ncol):
        x = x_ref[:, j * 2 * dc:(j + 1) * 2 * dc].astype(BF16)
        cs = _dot(m, x)
        ar = cs[:n1, :dc] - cs[n1:, dc:]
        ai = cs[:n1, dc:] + cs[n1:, :dc]
        ct = ct_ref[:, j * dc:(j + 1) * dc]
        st = st_ref[:, j * dc:(j + 1) * dc]
        br_ref[:, j * dc:(j + 1) * dc] = ct * ar - st * ai
        bi_ref[:, j * dc:(j + 1) * dc] = ct * ai + st * ar


def _dft_stage1_call(x2, m1, ct, st, dc, cols_per_step):
    b, n1, w = x2.shape
    steps = w // (cols_per_step * 2 * dc)
    return pl.pallas_call(
        functools.partial(_dft_stage1_kernel, dc=dc),
        grid=(steps, b),
        in_specs=[
            pl.BlockSpec((None, n1, cols_per_step * 2 * dc), lambda j, bi: (bi, 0, j)),
            pl.BlockSpec((2 * n1, n1), lambda j, bi: (0, 0)),
            pl.BlockSpec((n1, cols_per_step * dc), lambda j, bi: (0, j)),
            pl.BlockSpec((n1, cols_per_step * dc), lambda j, bi: (0, j)),
        ],
        out_specs=[pl.BlockSpec((None, n1, cols_per_step * dc), lambda j, bi: (bi, 0, j))] * 2,
        out_shape=[jax.ShapeDtypeStruct((b, n1, w // 2), F32)] * 2,
        compiler_params=_cparams("parallel", "parallel"),
        name="dft_stage1",
    )(x2, m1, ct, st)


def _dft_stage2_kernel(br_ref, bi_ref, c2_ref, s2_ref, y_ref):
    c2 = c2_ref[...]
    s2 = s2_ref[...]
    for k in range(br_ref.shape[0]):
        y_ref[k] = _dot(c2, br_ref[k].astype(BF16)) - _dot(s2, bi_ref[k].astype(BF16))


def _dft_stage2_call(br, bi, c2, s2, k1_per_step):
    b, n1, n2, dc = br.shape
    blk = pl.BlockSpec((None, k1_per_step, n2, dc), lambda bi_, k: (bi_, k, 0, 0))
    mat = pl.BlockSpec((n2, n2), lambda bi_, k: (0, 0))
    return pl.pallas_call(
        _dft_stage2_kernel,
        grid=(b, n1 // k1_per_step),
        in_specs=[blk, blk, mat, mat],
        out_specs=blk,
        out_shape=jax.ShapeDtypeStruct((b, n1, n2, dc), F32),
        compiler_params=_cparams("parallel", "parallel"),
        name="dft_stage2",
    )(br, bi, c2, s2)


def _dft_tables(l, scale):
    n2 = DFT_N2
    n1 = l // n2

    def cs(num, den):
        ang = (2.0 * np.pi / den) * (num % den).astype(np.float64)
        return np.cos(ang), np.sin(ang)

    k1 = np.arange(n1)
    c1, s1 = cs(np.outer(k1, k1), n1)
    l2 = np.arange(n2)
    ct, st = cs(np.outer(k1, l2), l)
    c2, s2 = cs(np.outer(l2, l2), n2)
    return c1, s1, ct, st, c2 * scale, s2 * scale


def _fourier_latent(pq, dc):
    b, l, _ = pq.shape
    n2 = DFT_N2
    n1 = l // n2
    c1, s1, ct, st, c2, s2 = _dft_tables(l, 1.0 / math.sqrt(l * HEAD_DIM))
    m1 = jnp.asarray(np.concatenate([c1, s1], axis=0), F32).astype(BF16)
    ct_e = jnp.asarray(np.repeat(ct, dc, axis=1), F32)
    st_e = jnp.asarray(np.repeat(st, dc, axis=1), F32)
    x2 = pq.reshape(b, n1, n2 * 2 * dc)
    br, bi = _dft_stage1_call(x2, m1, ct_e, st_e, dc, cols_per_step=min(8, n2))
    br = br.reshape(b, n1, n2, dc)
    bi = bi.reshape(b, n1, n2, dc)
    y = _dft_stage2_call(br, bi, jnp.asarray(c2, F32).astype(BF16), jnp.asarray(s2, F32).astype(BF16),
                         k1_per_step=min(8, n1))
    return jnp.swapaxes(y, 1, 2).reshape(b, l, dc)


def _fourier_direct(pq, dc):
    b, l, _ = pq.shape
    k = np.arange(l)
    ang = (2.0 * np.pi / l) * (np.outer(k, k) % l).astype(np.float64)
    scale = 1.0 / math.sqrt(l * HEAD_DIM)
    m = jnp.asarray(np.concatenate([np.cos(ang) * scale, -np.sin(ang) * scale], axis=1), F32).astype(BF16)
    return _dft_direct_call(pq, m)


def _mix_kernel(gate_ref, uv_ref, hf_ref, hb_ref, yc_ref, x_ref, mod_ref, seg_ref, ws_ref, sgb_ref,
                gmix_ref, wout_ref, g_ref, b_ref, o_ref, *, alpha, d_a, d_b):
    tile = x_ref.shape[0]
    ya = _gelu(gate_ref[...]) * (hf_ref[...] + hb_ref[...])

    z = _gelu(uv_ref[...])
    u = z[:, :d_b]
    v = z[:, d_b:]
    seg = seg_ref[...]
    inv = 1.0 / HEAD_DIM

    def seg_mean(t):
        hi = t.astype(BF16)
        lo = (t - hi.astype(F32)).astype(BF16)
        return (_dot(hi, seg) + _dot(lo, seg)) * inv

    dv = v - seg_mean(v)
    vn = dv * lax.rsqrt(seg_mean(dv * dv) + LN_EPS)
    lane = lax.broadcasted_iota(jnp.int32, (CHUNK, d_b), 1)
    heads = d_b // HEAD_DIM
    ws = ws_ref[...]
    sgb = sgb_ref[...]
    parts = []
    for c in range(tile // CHUNK):
        vc = vn[c * CHUNK:(c + 1) * CHUNK, :]
        rhs = jnp.concatenate(
            [jnp.where((lane >= hh * HEAD_DIM) & (lane < (hh + 1) * HEAD_DIM), vc, 0.0) for hh in range(heads)],
            axis=0).astype(BF16)
        parts.append(_dot(ws, rhs) + sgb)
    yb = u * jnp.concatenate(parts, axis=0)
    yc = yc_ref[...]

    gm = gmix_ref[...]

    def rms(t, g):
        return t * lax.rsqrt(jnp.mean(t * t, axis=-1, keepdims=True) + LN_EPS) * g

    y = jnp.concatenate([rms(ya, gm[:, :d_a]), rms(yb, gm[:, d_a:d_a + d_b]), rms(yc, gm[:, d_a + d_b:])], axis=-1)
    o = _dot(y.astype(BF16), wout_ref[...])
    m = mod_ref[...]
    o_ref[...] = _ln(alpha * x_ref[...] + m[2:3, :] * o) * g_ref[...] + b_ref[...]


def _mix_call(p_main, hf, hb, yc, x, mod, seg, ws_cat, sgb, g_mix, w_out, ln_g, ln_b, *, alpha, tile):
    b, l, d = x.shape
    d_a = hf.shape[-1]
    d_c = yc.shape[-1]
    d_b = d_c
    tok = lambda j: (lambda bi, i: (bi, i, j))
    vec = lambda shape: pl.BlockSpec(shape, lambda bi, i: (0,) * len(shape))
    return pl.pallas_call(
        functools.partial(_mix_kernel, alpha=alpha, d_a=d_a, d_b=d_b),
        grid=(b, l // tile),
        in_specs=[
            pl.BlockSpec((None, tile, d_a), tok(1)),
            pl.BlockSpec((None, tile, 2 * d_b), tok(2)),
            pl.BlockSpec((None, tile, d_a), tok(0)),
            pl.BlockSpec((None, tile, d_a), tok(0)),
            pl.BlockSpec((None, tile, d_c), tok(0)),
            pl.BlockSpec((None, tile, d), tok(0)),
            pl.BlockSpec((None, N_MOD, d), lambda bi, i: (bi, 0, 0)),
            vec(seg.shape), vec(ws_cat.shape), vec(sgb.shape), vec((1, d)), vec(w_out.shape),
            vec((1, d)), vec((1, d)),
        ],
        out_specs=pl.BlockSpec((None, tile, d), tok(0)),
        out_shape=jax.ShapeDtypeStruct((b, l, d), F32),
        compiler_params=_cparams("parallel", "parallel"),
        name="mix_out",
    )(p_main, p_main, hf, hb, yc, x, mod, seg, ws_cat, sgb, g_mix, w_out, ln_g, ln_b)


def _ffn_kernel(x_ref, mod_ref, wup_ref, wdn_ref, g_ref, b_ref, o_ref, acc_ref, *, alpha):
    x = x_ref[...]
    m = mod_ref[...]
    h = (_ln(x) * (1.0 + m[4:5, :]) + m[3:4, :]).astype(BF16)
    for j in range(wup_ref.shape[0]):
        gu = _dot(h, wup_ref[j])
        g = gu[:, :FF_CHUNK]
        act = (g * jax.nn.sigmoid(g) * gu[:, FF_CHUNK:]).astype(BF16)
        part = _dot(act, wdn_ref[j])
        if j == 0:
            acc_ref[...] = part
        else:
            acc_ref[...] += part
    o_ref[...] = _ln(alpha * x + m[5:6, :] * acc_ref[...]) * g_ref[...] + b_ref[...]


def _ffn_call(x, mod, w_up_r, w_dn_r, ln_g, ln_b, *, alpha, tile):
    b, l, d = x.shape
    tok = lambda bi, i: (bi, i, 0)
    vec = lambda shape: pl.BlockSpec(shape, lambda bi, i: (0,) * len(shape))
    return pl.pallas_call(
        functools.partial(_ffn_kernel, alpha=alpha),
        grid=(b, l // tile),
        in_specs=[
            pl.BlockSpec((None, tile, d), tok),
            pl.BlockSpec((None, N_MOD, d), lambda bi, i: (bi, 0, 0)),
            vec(w_up_r.shape), vec(w_dn_r.shape), vec((1, d)), vec((1, d)),
        ],
        out_specs=pl.BlockSpec((None, tile, d), tok),
        out_shape=jax.ShapeDtypeStruct((b, l, d), F32),
        scratch_shapes=[pltpu.VMEM((tile, d), F32)],
        compiler_params=_cparams("parallel", "parallel"),
        name="ffn",
    )(x, mod, w_up_r, w_dn_r, ln_g, ln_b)


def _pos_embed_2d(rows, dim):
    quarter = dim // 4
    freqs = POS_BASE ** (-jnp.arange(quarter, dtype=F32) / quarter)
    r = jnp.repeat(jnp.arange(rows, dtype=F32), GRID_W)
    col = jnp.tile(jnp.arange(GRID_W, dtype=F32), rows)

    def enc(p):
        ang = p[:, None] * freqs[None, :]
        return jnp.concatenate([jnp.sin(ang), jnp.cos(ang)], -1)

    return jnp.concatenate([enc(r), enc(col)], -1)


def _block_diag(blocks):
    n, r, c = blocks.shape
    eye = jnp.eye(n, dtype=blocks.dtype)
    return (eye[:, None, :, None] * blocks[:, :, None, :]).reshape(n * r, n * c)


def _gate_weights(wa, wx):
    heads = wa.shape[0]
    pa = jnp.stack([_block_diag(wa[2 * j:2 * j + 2]) for j in range(heads // 2)])
    px = jnp.stack([_block_diag(wx[2 * j:2 * j + 2]) for j in range(heads // 2)])
    return jnp.concatenate([pa, px], axis=-1).astype(BF16)


def kernel(x, c, ctx, c_ctx, w_mod, b_mod, w_in, conv_w, conv_b, lru_wa, lru_ba, lru_wx, lru_bx, lru_lam,
           sg_ws, sg_b, fourier_w, g_mix, w_out, ln1_g, ln1_b, w_up, w_down, ln2_g, ln2_b):
    bsz, seq, d = x.shape
    ctx_len = ctx.shape[1]
    depth = w_mod.shape[0]
    d_a = conv_w.shape[-1]
    d_c = fourier_w.shape[1] * fourier_w.shape[2]
    d_ff = w_down.shape[1]
    alpha = (2 * depth) ** 0.25
    lat_tile = min(512, seq)
    ctx_tile = min(256, ctx_len)

    cc = jnp.zeros((SUBLANES, d), F32).at[:bsz].set(c).at[bsz].set(c_ctx)
    mod = _mod_call(cc, w_mod, b_mod).reshape(depth, SUBLANES, N_MOD, d)

    pos = _pos_embed_2d(seq // GRID_W, d).astype(x.dtype)
    k = np.arange(HEAD_DIM)
    ang = (2.0 * np.pi / HEAD_DIM) * (np.outer(k, k) % HEAD_DIM).astype(np.float64)
    groups_c = d_c // HEAD_DIM
    cc_bd = jnp.asarray(np.kron(np.eye(groups_c), np.cos(ang)), F32)
    sc_bd = jnp.asarray(np.kron(np.eye(groups_c), np.sin(ang)), F32)
    seg = jnp.asarray(np.kron(np.eye(groups_c), np.ones((HEAD_DIM, HEAD_DIM))), BF16)
    zeros_state = jnp.zeros((bsz, SUBLANES, d_a), F32)

    xl, xc = x, ctx
    for l in range(depth):
        last = l == depth - 1
        mod_l = mod[l, :bsz]
        mod_c = jnp.broadcast_to(mod[l, bsz][None], (bsz, N_MOD, d))
        w_in_b = w_in[l].astype(BF16)
        wcs = _fold_call(cc_bd, sc_bd, _block_diag(fourier_w[l])).astype(BF16)
        cw, cb = conv_w[l], conv_b[l].reshape(1, d_a)
        lru = [(_gate_weights(lru_wa[l, dd], lru_wx[l, dd]), lru_ba[l, dd].reshape(1, d_a),
                lru_bx[l, dd].reshape(1, d_a), lru_lam[l, dd].reshape(1, d_a)) for dd in range(2)]
        ws_cat = jnp.concatenate([sg_ws[l, hh] for hh in range(sg_ws.shape[1])], axis=1).astype(BF16)
        sgb = jnp.repeat(jnp.transpose(sg_b[l]), HEAD_DIM, axis=1)
        w_out_b = w_out[l].astype(BF16)
        nff = d_ff // FF_CHUNK
        w_up_r = jnp.concatenate([w_up[l][:, :d_ff].reshape(d, nff, FF_CHUNK),
                                  w_up[l][:, d_ff:].reshape(d, nff, FF_CHUNK)], axis=-1)
        w_up_r = jnp.transpose(w_up_r, (1, 0, 2)).astype(BF16)
        w_dn_r = w_down[l].reshape(nff, FF_CHUNK, d).astype(BF16)
        g1, b1 = ln1_g[l].reshape(1, d), ln1_b[l].reshape(1, d)
        g2, b2 = ln2_g[l].reshape(1, d), ln2_b[l].reshape(1, d)
        gm = g_mix[l].reshape(1, d)

        def scans(p_main, h0f, h0b, tile):
            hf, sf = _lru_call(p_main, cw, cb, *lru[0], h0f, reverse=False, tile=tile)
            hb, sb = _lru_call(p_main, cw, cb, *lru[1], h0b, reverse=True, tile=tile)
            return hf, hb, sf, sb

        def mix_ffn(p_main, hf, hb, yc, xs, mods, tile):
            x1 = _mix_call(p_main, hf, hb, yc, xs, mods, seg, ws_cat, sgb, gm, w_out_b, g1, b1,
                           alpha=alpha, tile=tile)
            return _ffn_call(x1, mods, w_up_r, w_dn_r, g2, b2, alpha=alpha, tile=tile)

        pc_main, pc_pq = _inproj_call(xc, None, mod_c, w_in_b, wcs, ctx_tile)
        hf_c, hb_c, sf_c, sb_c = scans(pc_main, zeros_state, zeros_state, ctx_tile)
        if not last:
            xc = mix_ffn(pc_main, hf_c, hb_c, _fourier_direct(pc_pq, d_c), xc, mod_c, ctx_tile)

        if l == 0:
            xl, pl_main, pl_pq = _inproj_call(xl, pos, mod_l, w_in_b, wcs, lat_tile)
        else:
            pl_main, pl_pq = _inproj_call(xl, None, mod_l, w_in_b, wcs, lat_tile)
        hf, hb, _, _ = scans(pl_main, sf_c, sb_c, lat_tile)
        if seq > 2 * DFT_N2:
            yc = _fourier_latent(pl_pq, d_c)
        else:
            yc = _fourier_direct(pl_pq, d_c)
        xl = mix_ffn(pl_main, hf, hb, yc, xl, mod_l, lat_tile)
    return xl
```

```python
import functools
import math

import jax
import jax.numpy as jnp
import numpy as np
from jax import lax
from jax.experimental import pallas as pl
from jax.experimental.pallas import tpu as pltpu

HEAD_DIM = 64
CHUNK = 128
GRID_W = 64
RG_C = 8.0
LN_EPS = 1e-6
POS_BASE = 10000.0
N_MOD = 6
SUBLANES = 8
LANES = 128
FF_CHUNK = 256
DFT_N2 = 128
VMEM_LIMIT = 56 * 1024 * 1024

F32 = jnp.float32
BF16 = jnp.bfloat16


def _dot(a, b):
    return jnp.dot(a, b, preferred_element_type=F32)


def _ln(x):
    mu = jnp.mean(x, axis=-1, keepdims=True)
    d = x - mu
    var = jnp.mean(d * d, axis=-1, keepdims=True)
    return d * lax.rsqrt(var + LN_EPS)


def _gelu(x):
    return 0.5 * x * (1.0 + lax.erf(x * (1.0 / math.sqrt(2.0))))


def _cparams(*sem):
    return pltpu.CompilerParams(dimension_semantics=sem, vmem_limit_bytes=VMEM_LIMIT)


def _mod_kernel(c_ref, w_ref, b_ref, o_ref):
    c = c_ref[...]
    s = (c * jax.nn.sigmoid(c)).astype(BF16)
    o_ref[...] = _dot(s, w_ref[...].astype(BF16)) + b_ref[...]


def _mod_call(cc, w_mod, b_mod):
    depth, d, nm = w_mod.shape
    tn = 1536
    return pl.pallas_call(
        _mod_kernel,
        grid=(depth, nm // tn),
        in_specs=[
            pl.BlockSpec((SUBLANES, d), lambda l, j: (0, 0)),
            pl.BlockSpec((None, d, tn), lambda l, j: (l, 0, j)),
            pl.BlockSpec((None, 1, tn), lambda l, j: (l, 0, j)),
        ],
        out_specs=pl.BlockSpec((None, SUBLANES, tn), lambda l, j: (l, 0, j)),
        out_shape=jax.ShapeDtypeStruct((depth, SUBLANES, nm), F32),
        compiler_params=_cparams("parallel", "parallel"),
        name="mod_vectors",
    )(cc, w_mod, b_mod.reshape(depth, 1, nm))


def _split_bf16(t):
    hi = t.astype(BF16)
    return hi, (t - hi.astype(F32)).astype(BF16)


def _dot_3pass(a, b):
    a_hi, a_lo = _split_bf16(a)
    b_hi, b_lo = _split_bf16(b)
    return _dot(a_hi, b_hi) + (_dot(a_hi, b_lo) + _dot(a_lo, b_hi))


def _fold_kernel(cc_ref, sc_ref, wf_ref, o_ref):
    wf = wf_ref[...]
    n = wf.shape[0]
    o_ref[:, :n] = _dot_3pass(cc_ref[...], wf)
    o_ref[:, n:] = _dot_3pass(sc_ref[...], wf)


def _fold_call(cc_bd, sc_bd, wf_bd):
    n = wf_bd.shape[0]
    return pl.pallas_call(
        _fold_kernel,
        out_shape=jax.ShapeDtypeStruct((n, 2 * n), F32),
        name="fourier_weight_fold",
    )(cc_bd, sc_bd, wf_bd)


def _inproj_kernel(*refs, add_pos, d_main):
    if add_pos:
        x_ref, prow_ref, pcol_ref, mod_ref, w_ref, wcs_ref, xo_ref, pm_ref, pq_ref = refs
        tile, d = x_ref.shape
        half = d // 2
        nrow = tile // GRID_W
        xr = x_ref[:, :half].reshape(nrow, GRID_W, half) + prow_ref[...][:, None, :]
        xq = x_ref[:, half:].reshape(nrow, GRID_W, half) + pcol_ref[...][None, :, :]
        x = jnp.concatenate([xr.reshape(tile, half), xq.reshape(tile, half)], axis=-1)
        xo_ref[...] = x
    else:
        x_ref, mod_ref, w_ref, wcs_ref, pm_ref, pq_ref = refs
        x = x_ref[...]
    m = mod_ref[...]
    h = _ln(x) * (1.0 + m[1:2, :]) + m[0:1, :]
    p = _dot(h.astype(BF16), w_ref[...])
    pm_ref[...] = p[:, :d_main].astype(pm_ref.dtype)
    pq_ref[...] = _dot(p[:, d_main:].astype(BF16), wcs_ref[...]).astype(pq_ref.dtype)


def _inproj_call(x, pos, mod, w_in, wcs, tile):
    b, l, d = x.shape
    d_in = w_in.shape[1]
    d_c = wcs.shape[0]
    d_main = d_in - d_c
    add_pos = pos is not None
    tok = lambda bi, i: (bi, i, 0)
    in_specs = [pl.BlockSpec((None, tile, d), tok)]
    args = [x]
    if add_pos:
        prow, pcol = pos
        in_specs.append(pl.BlockSpec((tile // GRID_W, d // 2), lambda bi, i: (i, 0)))
        in_specs.append(pl.BlockSpec((GRID_W, d // 2), lambda bi, i: (0, 0)))
        args += [prow, pcol]
    in_specs += [
        pl.BlockSpec((None, N_MOD, d), lambda bi, i: (bi, 0, 0)),
        pl.BlockSpec((d, d_in), lambda bi, i: (0, 0)),
        pl.BlockSpec((d_c, 2 * d_c), lambda bi, i: (0, 0)),
    ]
    args += [mod, w_in, wcs]
    out_specs = [pl.BlockSpec((None, tile, d_main), tok), pl.BlockSpec((None, tile, 2 * d_c), tok)]
    out_shape = [jax.ShapeDtypeStruct((b, l, d_main), BF16), jax.ShapeDtypeStruct((b, l, 2 * d_c), BF16)]
    if add_pos:
        out_specs.insert(0, pl.BlockSpec((None, tile, d), tok))
        out_shape.insert(0, jax.ShapeDtypeStruct((b, l, d), F32))
    return pl.pallas_call(
        functools.partial(_inproj_kernel, add_pos=add_pos, d_main=d_main),
        grid=(b, l // tile),
        in_specs=in_specs,
        out_specs=out_specs,
        out_shape=out_shape,
        compiler_params=_cparams("parallel", "parallel"),
        name="in_proj",
    )(*args)


def _lru_kernel(x_ref, prev_ref, next_ref, cw_ref, cb_ref, wg_ref, ba_ref, bx_ref, lam_ref, h0_ref,
                h_ref, hfin_ref, slab_ref, xt_ref, gr_ref, gi_ref, hl_ref, al_ref, carry_ref, *, reverse, tile):
    i = pl.program_id(1)
    n = pl.num_programs(1)
    li = (n - 1 - i) if reverse else i
    d_a = x_ref.shape[-1]
    G = tile // SUBLANES
    nq = d_a // LANES
    pitch = slab_ref.shape[1] // SUBLANES

    @pl.when(i == 0)
    def _():
        carry_ref[...] = h0_ref[...]

    def to_slabs(val):
        for s in range(SUBLANES):
            for q in range(nq):
                slab_ref[q, s * pitch:s * pitch + G, :] = val[s * G:(s + 1) * G, q * LANES:(q + 1) * LANES]

    to_slabs(x_ref[...].astype(F32))

    def load_t(g, c):
        for q in range(nq):
            xt_ref[g + 2, :, q * LANES:(q + 1) * LANES] = slab_ref[q, pl.ds(g, SUBLANES, stride=pitch), :]
        return c

    lax.fori_loop(0, G, load_t, 0, unroll=8)
    sub = lax.broadcasted_iota(jnp.int32, (SUBLANES, d_a), 0)
    prev = jnp.where(li == 0, 0.0, prev_ref[...].astype(F32))
    nxt = jnp.where(li == n - 1, 0.0, next_ref[...].astype(F32))
    hrows = prev_ref.shape[0]
    bc = lambda row: jnp.broadcast_to(row, (SUBLANES, d_a))
    xt_ref[0] = jnp.where(sub == 0, bc(prev[hrows - 2:hrows - 1, :]), pltpu.roll(xt_ref[G], 1, 0))
    xt_ref[1] = jnp.where(sub == 0, bc(prev[hrows - 1:hrows, :]), pltpu.roll(xt_ref[G + 1], 1, 0))
    xt_ref[G + 2] = jnp.where(sub == SUBLANES - 1, bc(nxt[0:1, :]), pltpu.roll(xt_ref[2], SUBLANES - 1, 0))

    cw = cw_ref[...]
    xh = (cb_ref[...][None] + cw[0:1, :][None] * xt_ref[0:G] + cw[1:2, :][None] * xt_ref[1:G + 1]
          + cw[2:3, :][None] * xt_ref[2:G + 2] + cw[3:4, :][None] * xt_ref[3:G + 3])
    xt_ref[0:G] = xh
    xhb = xh.reshape(tile, d_a).astype(BF16)
    pair = 2 * HEAD_DIM
    for j in range(d_a // pair):
        g2 = _dot(xhb[:, j * pair:(j + 1) * pair], wg_ref[j])
        gr_ref[:, :, j * pair:(j + 1) * pair] = g2[:, :pair].reshape(G, SUBLANES, pair)
        gi_ref[:, :, j * pair:(j + 1) * pair] = g2[:, pair:].reshape(G, SUBLANES, pair)

    lam = lam_ref[...]
    neg = -lam
    softplus = jnp.maximum(neg, 0.0) + jnp.log1p(jnp.exp(-jnp.abs(neg)))
    k2 = bc((-0.5 * RG_C / math.log(2.0)) * softplus)
    hba = bc(ba_ref[...])
    hbx = bc(bx_ref[...])

    def step(t, carry):
        h, acum = carry
        g = (G - 1 - t) if reverse else t
        t_r = jnp.tanh(gr_ref[g] + hba)
        t_i = jnp.tanh(gi_ref[g] + hbx)
        l2 = k2 + k2 * t_r
        a = jnp.exp2(l2)
        w = jnp.tanh(l2 * (-math.log(2.0))) * (1.0 + a * a)
        xg = xt_ref[g]
        u = jnp.sqrt(w) * (xg * t_i + xg)
        h = a * h + u
        acum = acum * a
        hl_ref[g] = h
        al_ref[g] = acum
        return h, acum

    zeros = jnp.zeros((SUBLANES, d_a), F32)
    h_end, a_end = lax.fori_loop(0, G, step, (zeros, zeros + 1.0), unroll=8)

    first = SUBLANES - 1 if reverse else 0
    shift = SUBLANES - 1 if reverse else 1
    c_in = carry_ref[...]
    c = c_in
    for _ in range(SUBLANES - 1):
        c = jnp.where(sub == first, c_in, pltpu.roll(h_end + a_end * c, shift, 0))
    out = h_end + a_end * c
    last = SUBLANES - 1 - first
    carry = bc(out[last:last + 1, :])
    carry_ref[...] = carry
    hfin_ref[...] = carry

    hfull = hl_ref[...] + al_ref[...] * c[None]
    hl_ref[...] = hfull

    def store_t(g, cc):
        for q in range(nq):
            slab_ref[q, pl.ds(g, SUBLANES, stride=pitch), :] = hl_ref[g, :, q * LANES:(q + 1) * LANES]
        return cc

    lax.fori_loop(0, G, store_t, 0, unroll=8)
    for s in range(SUBLANES):
        for q in range(nq):
            h_ref[s * G:(s + 1) * G, q * LANES:(q + 1) * LANES] = (
                slab_ref[q, s * pitch:s * pitch + G, :].astype(h_ref.dtype))


def _lru_call(p_main, conv_w_half, conv_b_half, wg, ba_half, bx_half, lam, h0, *, reverse, tile):
    b, l, _ = p_main.shape
    d_a = conv_w_half.shape[-1]
    n = l // tile
    halo = 2 * SUBLANES
    per = tile // halo
    nblk = l // halo
    G = tile // SUBLANES
    pitch = G + SUBLANES
    pos = (lambda i: n - 1 - i) if reverse else (lambda i: i)
    vec = lambda shape: pl.BlockSpec(shape, lambda bi, i: (0,) * len(shape))
    return pl.pallas_call(
        functools.partial(_lru_kernel, reverse=reverse, tile=tile),
        grid=(b, n),
        in_specs=[
            pl.BlockSpec((None, tile, d_a), lambda bi, i: (bi, pos(i), 0)),
            pl.BlockSpec((None, halo, d_a), lambda bi, i: (bi, jnp.maximum(pos(i) * per - 1, 0), 0)),
            pl.BlockSpec((None, halo, d_a), lambda bi, i: (bi, jnp.minimum((pos(i) + 1) * per, nblk - 1), 0)),
            vec((4, d_a)), vec((1, d_a)), vec(wg.shape), vec((1, d_a)), vec((1, d_a)), vec((1, d_a)),
            pl.BlockSpec((None, SUBLANES, d_a), lambda bi, i: (bi, 0, 0)),
        ],
        out_specs=[
            pl.BlockSpec((None, tile, d_a), lambda bi, i: (bi, pos(i), 0)),
            pl.BlockSpec((None, SUBLANES, d_a), lambda bi, i: (bi, 0, 0)),
        ],
        out_shape=[jax.ShapeDtypeStruct((b, l, d_a), BF16), jax.ShapeDtypeStruct((b, SUBLANES, d_a), F32)],
        scratch_shapes=[
            pltpu.VMEM((d_a // LANES, SUBLANES * pitch, LANES), F32),
            pltpu.VMEM((G + 3, SUBLANES, d_a), F32),
            pltpu.VMEM((G, SUBLANES, d_a), F32),
            pltpu.VMEM((G, SUBLANES, d_a), F32),
            pltpu.VMEM((G, SUBLANES, d_a), F32),
            pltpu.VMEM((G, SUBLANES, d_a), F32),
            pltpu.VMEM((SUBLANES, d_a), F32),
        ],
        compiler_params=_cparams("arbitrary", "arbitrary"),
        name="lru_scan_bwd" if reverse else "lru_scan_fwd",
    )(p_main, p_main, p_main, conv_w_half, conv_b_half, wg, ba_half, bx_half, lam, h0)


def _dft_direct_kernel(pq_ref, m_ref, y_ref):
    pq = pq_ref[...]
    dc = pq.shape[-1] // 2
    rhs = jnp.concatenate([pq[:, :dc], pq[:, dc:]], axis=0).astype(BF16)
    y_ref[...] = _dot(m_ref[...], rhs).astype(y_ref.dtype)


def _dft_direct_call(pq, m):
    b, l, dc2 = pq.shape
    return pl.pallas_call(
        _dft_direct_kernel,
        grid=(b,),
        in_specs=[pl.BlockSpec((None, l, dc2), lambda bi: (bi, 0, 0)),
                  pl.BlockSpec((l, 2 * l), lambda bi: (0, 0))],
        out_specs=pl.BlockSpec((None, l, dc2 // 2), lambda bi: (bi, 0, 0)),
        out_shape=jax.ShapeDtypeStruct((b, l, dc2 // 2), BF16),
        compiler_params=_cparams("parallel"),
        name="dft_direct",
    )(pq, m)


def _dft_stage1_kernel(x_ref, m_ref, ct_ref, st_ref, br_ref, bi_ref, *, dc):
    n1 = x_ref.shape[0]
    m = m_ref[...]
    ncol = x_ref.shape[1] // (2 * dc)
    for j in range(ncol):
        x = x_ref[:, j * 2 * dc:(j + 1) * 2 * dc].astype(BF16)
        cs = _dot(m, x)
        ar = cs[:n1, :dc] - cs[n1:, dc:]
        ai = cs[:n1, dc:] + cs[n1:, :dc]
        ct = ct_ref[:, j * dc:(j + 1) * dc]
        st = st_ref[:, j * dc:(j + 1) * dc]
        br_ref[:, j * dc:(j + 1) * dc] = (ct * ar - st * ai).astype(br_ref.dtype)
        bi_ref[:, j * dc:(j + 1) * dc] = (ct * ai + st * ar).astype(bi_ref.dtype)


def _dft_stage1_call(x2, m1, ct, st, dc, cols_per_step):
    b, n1, w = x2.shape
    steps = w // (cols_per_step * 2 * dc)
    return pl.pallas_call(
        functools.partial(_dft_stage1_kernel, dc=dc),
        grid=(steps, b),
        in_specs=[
            pl.BlockSpec((None, n1, cols_per_step * 2 * dc), lambda j, bi: (bi, 0, j)),
            pl.BlockSpec((2 * n1, n1), lambda j, bi: (0, 0)),
            pl.BlockSpec((n1, cols_per_step * dc), lambda j, bi: (0, j)),
            pl.BlockSpec((n1, cols_per_step * dc), lambda j, bi: (0, j)),
        ],
        out_specs=[pl.BlockSpec((None, n1, cols_per_step * dc), lambda j, bi: (bi, 0, j))] * 2,
        out_shape=[jax.ShapeDtypeStruct((b, n1, w // 2), BF16)] * 2,
        compiler_params=_cparams("parallel", "parallel"),
        name="dft_stage1",
    )(x2, m1, ct, st)


def _dft_stage2_kernel(br_ref, bi_ref, c2_ref, s2_ref, y_ref):
    c2 = c2_ref[...]
    s2 = s2_ref[...]
    for k in range(br_ref.shape[0]):
        y_ref[k] = (_dot(c2, br_ref[k].astype(BF16)) - _dot(s2, bi_ref[k].astype(BF16))).astype(y_ref.dtype)


def _dft_stage2_call(br, bi, c2, s2, k1_per_step):
    b, n1, n2, dc = br.shape
    blk = pl.BlockSpec((None, k1_per_step, n2, dc), lambda bi_, k: (bi_, k, 0, 0))
    mat = pl.BlockSpec((n2, n2), lambda bi_, k: (0, 0))
    return pl.pallas_call(
        _dft_stage2_kernel,
        grid=(b, n1 // k1_per_step),
        in_specs=[blk, blk, mat, mat],
        out_specs=blk,
        out_shape=jax.ShapeDtypeStruct((b, n1, n2, dc), BF16),
        compiler_params=_cparams("parallel", "parallel"),
        name="dft_stage2",
    )(br, bi, c2, s2)


def _dft_tables(l, scale):
    n2 = DFT_N2
    n1 = l // n2

    def cs(num, den):
        ang = (2.0 * np.pi / den) * (num % den).astype(np.float64)
        return np.cos(ang), np.sin(ang)

    k1 = np.arange(n1)
    c1, s1 = cs(np.outer(k1, k1), n1)
    l2 = np.arange(n2)
    ct, st = cs(np.outer(k1, l2), l)
    c2, s2 = cs(np.outer(l2, l2), n2)
    return c1, s1, ct, st, c2 * scale, s2 * scale


def _fourier_latent(pq, dc):
    b, l, _ = pq.shape
    n2 = DFT_N2
    n1 = l // n2
    c1, s1, ct, st, c2, s2 = _dft_tables(l, 1.0 / math.sqrt(l * HEAD_DIM))
    m1 = jnp.asarray(np.concatenate([c1, s1], axis=0), F32).astype(BF16)
    ct_e = jnp.asarray(np.repeat(ct, dc, axis=1), F32)
    st_e = jnp.asarray(np.repeat(st, dc, axis=1), F32)
    x2 = pq.reshape(b, n1, n2 * 2 * dc)
    br, bi = _dft_stage1_call(x2, m1, ct_e, st_e, dc, cols_per_step=min(8, n2))
    br = br.reshape(b, n1, n2, dc)
    bi = bi.reshape(b, n1, n2, dc)
    y = _dft_stage2_call(br, bi, jnp.asarray(c2, F32).astype(BF16), jnp.asarray(s2, F32).astype(BF16),
                         k1_per_step=min(8, n1))
    return jnp.swapaxes(y, 1, 2).reshape(b, l, dc)


def _fourier_direct(pq, dc):
    b, l, _ = pq.shape
    k = np.arange(l)
    ang = (2.0 * np.pi / l) * (np.outer(k, k) % l).astype(np.float64)
    scale = 1.0 / math.sqrt(l * HEAD_DIM)
    m = jnp.asarray(np.concatenate([np.cos(ang) * scale, -np.sin(ang) * scale], axis=1), F32).astype(BF16)
    return _dft_direct_call(pq, m)


def _mix_kernel(gate_ref, uv_ref, hf_ref, hb_ref, yc_ref, x_ref, mod_ref, seg_ref, ws_ref, sgb_ref,
                gmix_ref, wout_ref, g_ref, b_ref, o_ref, *, alpha, d_a, d_b):
    tile = x_ref.shape[0]
    ya = _gelu(gate_ref[...].astype(F32)) * (hf_ref[...].astype(F32) + hb_ref[...].astype(F32))

    z = _gelu(uv_ref[...].astype(F32))
    u = z[:, :d_b]
    v = z[:, d_b:]
    seg = seg_ref[...]
    inv = 1.0 / HEAD_DIM

    def seg_mean(t):
        hi = t.astype(BF16)
        lo = (t - hi.astype(F32)).astype(BF16)
        return (_dot(hi, seg) + _dot(lo, seg)) * inv

    dv = v - seg_mean(v)
    vn = dv * lax.rsqrt(seg_mean(dv * dv) + LN_EPS)
    lane = lax.broadcasted_iota(jnp.int32, (CHUNK, d_b), 1)
    heads = d_b // HEAD_DIM
    ws = ws_ref[...]
    sgb = sgb_ref[...]
    parts = []
    for c in range(tile // CHUNK):
        vc = vn[c * CHUNK:(c + 1) * CHUNK, :]
        rhs = jnp.concatenate(
            [jnp.where((lane >= hh * HEAD_DIM) & (lane < (hh + 1) * HEAD_DIM), vc, 0.0) for hh in range(heads)],
            axis=0).astype(BF16)
        parts.append(_dot(ws, rhs) + sgb)
    yb = u * jnp.concatenate(parts, axis=0)
    yc = yc_ref[...].astype(F32)

    gm = gmix_ref[...]

    def rms(t, g):
        return t * lax.rsqrt(jnp.mean(t * t, axis=-1, keepdims=True) + LN_EPS) * g

    y = jnp.concatenate([rms(ya, gm[:, :d_a]), rms(yb, gm[:, d_a:d_a + d_b]), rms(yc, gm[:, d_a + d_b:])], axis=-1)
    o = _dot(y.astype(BF16), wout_ref[...])
    m = mod_ref[...]
    o_ref[...] = _ln(alpha * x_ref[...] + m[2:3, :] * o) * g_ref[...] + b_ref[...]


def _mix_call(p_main, hf, hb, yc, x, mod, seg, ws_cat, sgb, g_mix, w_out, ln_g, ln_b, *, alpha, tile):
    b, l, d = x.shape
    d_a = hf.shape[-1]
    d_c = yc.shape[-1]
    d_b = d_c
    tok = lambda j: (lambda bi, i: (bi, i, j))
    vec = lambda shape: pl.BlockSpec(shape, lambda bi, i: (0,) * len(shape))
    return pl.pallas_call(
        functools.partial(_mix_kernel, alpha=alpha, d_a=d_a, d_b=d_b),
        grid=(b, l // tile),
        in_specs=[
            pl.BlockSpec((None, tile, d_a), tok(1)),
            pl.BlockSpec((None, tile, 2 * d_b), tok(2)),
            pl.BlockSpec((None, tile, d_a), tok(0)),
            pl.BlockSpec((None, tile, d_a), tok(0)),
            pl.BlockSpec((None, tile, d_c), tok(0)),
            pl.BlockSpec((None, tile, d), tok(0)),
            pl.BlockSpec((None, N_MOD, d), lambda bi, i: (bi, 0, 0)),
            vec(seg.shape), vec(ws_cat.shape), vec(sgb.shape), vec((1, d)), vec(w_out.shape),
            vec((1, d)), vec((1, d)),
        ],
        out_specs=pl.BlockSpec((None, tile, d), tok(0)),
        out_shape=jax.ShapeDtypeStruct((b, l, d), F32),
        compiler_params=_cparams("parallel", "parallel"),
        name="mix_out",
    )(p_main, p_main, hf, hb, yc, x, mod, seg, ws_cat, sgb, g_mix, w_out, ln_g, ln_b)


def _ffn_kernel(x_ref, mod_ref, wup_ref, wdn_ref, g_ref, b_ref, o_ref, acc_ref, *, alpha):
    x = x_ref[...]
    m = mod_ref[...]
    h = (_ln(x) * (1.0 + m[4:5, :]) + m[3:4, :]).astype(BF16)
    for j in range(wup_ref.shape[0]):
        gu = _dot(h, wup_ref[j])
        g = gu[:, :FF_CHUNK]
        act = (g * jax.nn.sigmoid(g) * gu[:, FF_CHUNK:]).astype(BF16)
        part = _dot(act, wdn_ref[j])
        if j == 0:
            acc_ref[...] = part
        else:
            acc_ref[...] += part
    o_ref[...] = _ln(alpha * x + m[5:6, :] * acc_ref[...]) * g_ref[...] + b_ref[...]


def _ffn_call(x, mod, w_up_r, w_dn_r, ln_g, ln_b, *, alpha, tile):
    b, l, d = x.shape
    tok = lambda bi, i: (bi, i, 0)
    vec = lambda shape: pl.BlockSpec(shape, lambda bi, i: (0,) * len(shape))
    res = lambda shape: pl.BlockSpec(shape, lambda bi, i: (0,) * len(shape), pipeline_mode=pl.Buffered(1))
    return pl.pallas_call(
        functools.partial(_ffn_kernel, alpha=alpha),
        grid=(b, l // tile),
        in_specs=[
            pl.BlockSpec((None, tile, d), tok),
            pl.BlockSpec((None, N_MOD, d), lambda bi, i: (bi, 0, 0)),
            res(w_up_r.shape), res(w_dn_r.shape), vec((1, d)), vec((1, d)),
        ],
        out_specs=pl.BlockSpec((None, tile, d), tok),
        out_shape=jax.ShapeDtypeStruct((b, l, d), F32),
        scratch_shapes=[pltpu.VMEM((tile, d), F32)],
        compiler_params=_cparams("parallel", "parallel"),
        name="ffn",
    )(x, mod, w_up_r, w_dn_r, ln_g, ln_b)


def _pos_tables(rows, dim):
    quarter = dim // 4
    freqs = POS_BASE ** (-jnp.arange(quarter, dtype=F32) / quarter)

    def enc(p):
        ang = p[:, None] * freqs[None, :]
        return jnp.concatenate([jnp.sin(ang), jnp.cos(ang)], -1)

    return enc(jnp.arange(rows, dtype=F32)), enc(jnp.arange(GRID_W, dtype=F32))


def _block_diag(blocks):
    n, r, c = blocks.shape
    eye = jnp.eye(n, dtype=blocks.dtype)
    return (eye[:, None, :, None] * blocks[:, :, None, :]).reshape(n * r, n * c)


def _gate_weights(wa, wx):
    heads = wa.shape[0]
    pa = jnp.stack([_block_diag(wa[2 * j:2 * j + 2]) for j in range(heads // 2)])
    px = jnp.stack([_block_diag(wx[2 * j:2 * j + 2]) for j in range(heads // 2)])
    return jnp.concatenate([pa, px], axis=-1).astype(BF16)


def kernel(x, c, ctx, c_ctx, w_mod, b_mod, w_in, conv_w, conv_b, lru_wa, lru_ba, lru_wx, lru_bx, lru_lam,
           sg_ws, sg_b, fourier_w, g_mix, w_out, ln1_g, ln1_b, w_up, w_down, ln2_g, ln2_b):
    bsz, seq, d = x.shape
    ctx_len = ctx.shape[1]
    depth = w_mod.shape[0]
    d_a = conv_w.shape[-1]
    d_c = fourier_w.shape[1] * fourier_w.shape[2]
    d_ff = w_down.shape[1]
    alpha = (2 * depth) ** 0.25
    lat_tile = min(512, seq)
    ffn_tile = min(1024, seq)
    ctx_tile = min(256, ctx_len)

    cc = jnp.zeros((SUBLANES, d), F32).at[:bsz].set(c).at[bsz].set(c_ctx)
    mod = _mod_call(cc, w_mod, b_mod).reshape(depth, SUBLANES, N_MOD, d)

    pos = _pos_tables(seq // GRID_W, d)
    k = np.arange(HEAD_DIM)
    ang = (2.0 * np.pi / HEAD_DIM) * (np.outer(k, k) % HEAD_DIM).astype(np.float64)
    groups_c = d_c // HEAD_DIM
    cc_bd = jnp.asarray(np.kron(np.eye(groups_c), np.cos(ang)), F32)
    sc_bd = jnp.asarray(np.kron(np.eye(groups_c), np.sin(ang)), F32)
    seg = jnp.asarray(np.kron(np.eye(groups_c), np.ones((HEAD_DIM, HEAD_DIM))), BF16)
    zeros_state = jnp.zeros((bsz, SUBLANES, d_a), F32)

    xl, xc = x, ctx
    for l in range(depth):
        last = l == depth - 1
        mod_l = mod[l, :bsz]
        mod_c = jnp.broadcast_to(mod[l, bsz][None], (bsz, N_MOD, d))
        w_in_b = w_in[l].astype(BF16)
        wcs = _fold_call(cc_bd, sc_bd, _block_diag(fourier_w[l])).astype(BF16)
        cw, cb = 0.5 * conv_w[l], 0.5 * conv_b[l].reshape(1, d_a)
        lru = [(_gate_weights(lru_wa[l, dd], lru_wx[l, dd]), 0.5 * lru_ba[l, dd].reshape(1, d_a),
                0.5 * lru_bx[l, dd].reshape(1, d_a), lru_lam[l, dd].reshape(1, d_a)) for dd in range(2)]
        ws_cat = jnp.concatenate([sg_ws[l, hh] for hh in range(sg_ws.shape[1])], axis=1).astype(BF16)
        sgb = jnp.repeat(jnp.transpose(sg_b[l]), HEAD_DIM, axis=1)
        w_out_b = w_out[l].astype(BF16)
        nff = d_ff // FF_CHUNK
        w_up_r = jnp.concatenate([w_up[l][:, :d_ff].reshape(d, nff, FF_CHUNK),
                                  w_up[l][:, d_ff:].reshape(d, nff, FF_CHUNK)], axis=-1)
        w_up_r = jnp.transpose(w_up_r, (1, 0, 2)).astype(BF16)
        w_dn_r = w_down[l].reshape(nff, FF_CHUNK, d).astype(BF16)
        g1, b1 = ln1_g[l].reshape(1, d), ln1_b[l].reshape(1, d)
        g2, b2 = ln2_g[l].reshape(1, d), ln2_b[l].reshape(1, d)
        gm = g_mix[l].reshape(1, d)

        def scans(p_main, h0f, h0b, tile):
            hf, sf = _lru_call(p_main, cw, cb, *lru[0], h0f, reverse=False, tile=tile)
            hb, sb = _lru_call(p_main, cw, cb, *lru[1], h0b, reverse=True, tile=tile)
            return hf, hb, sf, sb

        def mix_ffn(p_main, hf, hb, yc, xs, mods, tile, tile_ffn):
            x1 = _mix_call(p_main, hf, hb, yc, xs, mods, seg, ws_cat, sgb, gm, w_out_b, g1, b1,
                           alpha=alpha, tile=tile)
            return _ffn_call(x1, mods, w_up_r, w_dn_r, g2, b2, alpha=alpha, tile=tile_ffn)

        pc_main, pc_pq = _inproj_call(xc, None, mod_c, w_in_b, wcs, ctx_tile)
        hf_c, hb_c, sf_c, sb_c = scans(pc_main, zeros_state, zeros_state, ctx_tile)
        if not last:
            xc = mix_ffn(pc_main, hf_c, hb_c, _fourier_direct(pc_pq, d_c), xc, mod_c, ctx_tile, ctx_tile)

        if l == 0:
            xl, pl_main, pl_pq = _inproj_call(xl, pos, mod_l, w_in_b, wcs, lat_tile)
        else:
            pl_main, pl_pq = _inproj_call(xl, None, mod_l, w_in_b, wcs, lat_tile)
        hf, hb, _, _ = scans(pl_main, sf_c, sb_c, lat_tile)
        if seq > 2 * DFT_N2:
            yc = _fourier_latent(pl_pq, d_c)
        else:
            yc = _fourier_direct(pl_pq, d_c)
        xl = mix_ffn(pl_main, hf, hb, yc, xl, mod_l, lat_tile, ffn_tile)
    return xl
```

```python
import functools
import math

import jax
import jax.numpy as jnp
import numpy as np
from jax import lax
from jax.experimental import pallas as pl
from jax.experimental.pallas import tpu as pltpu

HEAD_DIM = 64
CHUNK = 128
GRID_W = 64
RG_C = 8.0
LN_EPS = 1e-6
POS_BASE = 10000.0
N_MOD = 6
SUBLANES = 8
LANES = 128
FF_CHUNK = 256
DFT_N2 = 128
INPROJ_ROWS = 512
VMEM_LIMIT = 56 * 1024 * 1024

F32 = jnp.float32
BF16 = jnp.bfloat16


def _dot(a, b):
    return jnp.dot(a, b, preferred_element_type=F32)


def _ln(x):
    mu = jnp.mean(x, axis=-1, keepdims=True)
    d = x - mu
    var = jnp.mean(d * d, axis=-1, keepdims=True)
    return d * lax.rsqrt(var + LN_EPS)


def _gelu(x):
    return 0.5 * x * (1.0 + lax.erf(x * (1.0 / math.sqrt(2.0))))


def _cparams(*sem):
    return pltpu.CompilerParams(dimension_semantics=sem, vmem_limit_bytes=VMEM_LIMIT)


def _mod_kernel(c_ref, w_ref, b_ref, o_ref):
    c = c_ref[...]
    s = (c * jax.nn.sigmoid(c)).astype(BF16)
    o_ref[...] = _dot(s, w_ref[...].astype(BF16)) + b_ref[...]


def _mod_call(cc, w_mod, b_mod):
    depth, d, nm = w_mod.shape
    tn = 1536
    return pl.pallas_call(
        _mod_kernel,
        grid=(depth, nm // tn),
        in_specs=[
            pl.BlockSpec((SUBLANES, d), lambda l, j: (0, 0)),
            pl.BlockSpec((None, d, tn), lambda l, j: (l, 0, j)),
            pl.BlockSpec((None, 1, tn), lambda l, j: (l, 0, j)),
        ],
        out_specs=pl.BlockSpec((None, SUBLANES, tn), lambda l, j: (l, 0, j)),
        out_shape=jax.ShapeDtypeStruct((depth, SUBLANES, nm), F32),
        compiler_params=_cparams("parallel", "parallel"),
        name="mod_vectors",
    )(cc, w_mod, b_mod.reshape(depth, 1, nm))


def _split_bf16(t):
    hi = t.astype(BF16)
    return hi, (t - hi.astype(F32)).astype(BF16)


def _dot_3pass(a, b):
    a_hi, a_lo = _split_bf16(a)
    b_hi, b_lo = _split_bf16(b)
    return _dot(a_hi, b_hi) + (_dot(a_hi, b_lo) + _dot(a_lo, b_hi))


def _fold_kernel(cc_ref, sc_ref, wf_ref, o_ref):
    wf = wf_ref[...]
    n = wf.shape[0]
    o_ref[:, :n] = _dot_3pass(cc_ref[...], wf)
    o_ref[:, n:] = _dot_3pass(sc_ref[...], wf)


def _fold_call(cc_bd, sc_bd, wf_bd):
    n = wf_bd.shape[0]
    return pl.pallas_call(
        _fold_kernel,
        out_shape=jax.ShapeDtypeStruct((n, 2 * n), F32),
        name="fourier_weight_fold",
    )(cc_bd, sc_bd, wf_bd)


def _inproj_kernel(*refs, add_pos, d_main):
    if add_pos:
        x_ref, prow_ref, pcol_ref, mod_ref, w_ref, wcs_ref, xo_ref, pm_ref, pq_ref = refs
    else:
        x_ref, mod_ref, w_ref, wcs_ref, pm_ref, pq_ref = refs
    tile, d = x_ref.shape
    m = mod_ref[...]
    sub = min(tile, INPROJ_ROWS)

    def normed(k):
        rows = pl.ds(k * sub, sub)
        if add_pos:
            half = d // 2
            nrow = sub // GRID_W
            prow = prow_ref[pl.ds(k * nrow, nrow), :]
            xr = x_ref[rows, :half].reshape(nrow, GRID_W, half) + prow[:, None, :]
            xq = x_ref[rows, half:].reshape(nrow, GRID_W, half) + pcol_ref[...][None, :, :]
            x = jnp.concatenate([xr.reshape(sub, half), xq.reshape(sub, half)], axis=-1)
            xo_ref[rows, :] = x
        else:
            x = x_ref[rows, :]
        return (_ln(x) * (1.0 + m[1:2, :]) + m[0:1, :]).astype(BF16)

    nsub = tile // sub
    h = normed(0)
    for k in range(nsub):
        rows = pl.ds(k * sub, sub)
        p = _dot(h, w_ref[...])
        if k + 1 < nsub:
            h = normed(k + 1)
        pm_ref[rows, :] = p[:, :d_main].astype(pm_ref.dtype)
        pq_ref[rows, :] = _dot(p[:, d_main:].astype(BF16), wcs_ref[...]).astype(pq_ref.dtype)


def _inproj_call(x, pos, mod, w_in, wcs, tile):
    b, l, d = x.shape
    d_in = w_in.shape[1]
    d_c = wcs.shape[0]
    d_main = d_in - d_c
    add_pos = pos is not None
    tok = lambda bi, i: (bi, i, 0)
    in_specs = [pl.BlockSpec((None, tile, d), tok)]
    args = [x]
    if add_pos:
        prow, pcol = pos
        in_specs.append(pl.BlockSpec((tile // GRID_W, d // 2), lambda bi, i: (i, 0)))
        in_specs.append(pl.BlockSpec((GRID_W, d // 2), lambda bi, i: (0, 0)))
        args += [prow, pcol]
    in_specs += [
        pl.BlockSpec((None, N_MOD, d), lambda bi, i: (bi, 0, 0)),
        pl.BlockSpec((d, d_in), lambda bi, i: (0, 0)),
        pl.BlockSpec((d_c, 2 * d_c), lambda bi, i: (0, 0)),
    ]
    args += [mod, w_in, wcs]
    out_specs = [pl.BlockSpec((None, tile, d_main), tok), pl.BlockSpec((None, tile, 2 * d_c), tok)]
    out_shape = [jax.ShapeDtypeStruct((b, l, d_main), BF16), jax.ShapeDtypeStruct((b, l, 2 * d_c), BF16)]
    if add_pos:
        out_specs.insert(0, pl.BlockSpec((None, tile, d), tok))
        out_shape.insert(0, jax.ShapeDtypeStruct((b, l, d), F32))
    return pl.pallas_call(
        functools.partial(_inproj_kernel, add_pos=add_pos, d_main=d_main),
        grid=(b, l // tile),
        in_specs=in_specs,
        out_specs=out_specs,
        out_shape=out_shape,
        compiler_params=_cparams("parallel", "parallel"),
        name="in_proj",
    )(*args)


def _lru_kernel(x_ref, prev_ref, next_ref, cw_ref, cb_ref, wg_ref, ba_ref, bx_ref, lam_ref, h0_ref,
                h_ref, hfin_ref, slab_ref, xt_ref, gr_ref, gi_ref, hl_ref, al_ref, carry_ref, *, reverse, tile):
    i = pl.program_id(1)
    n = pl.num_programs(1)
    li = (n - 1 - i) if reverse else i
    d_a = x_ref.shape[-1]
    G = tile // SUBLANES
    nq = d_a // LANES
    pitch = slab_ref.shape[1] // SUBLANES

    @pl.when(i == 0)
    def _():
        carry_ref[...] = h0_ref[...]

    def to_slabs(val):
        for s in range(SUBLANES):
            for q in range(nq):
                slab_ref[q, s * pitch:s * pitch + G, :] = val[s * G:(s + 1) * G, q * LANES:(q + 1) * LANES]

    to_slabs(x_ref[...].astype(F32))

    def load_t(g, c):
        for q in range(nq):
            xt_ref[g + 2, :, q * LANES:(q + 1) * LANES] = slab_ref[q, pl.ds(g, SUBLANES, stride=pitch), :]
        return c

    lax.fori_loop(0, G, load_t, 0, unroll=8)
    sub = lax.broadcasted_iota(jnp.int32, (SUBLANES, d_a), 0)
    prev = jnp.where(li == 0, 0.0, prev_ref[...].astype(F32))
    nxt = jnp.where(li == n - 1, 0.0, next_ref[...].astype(F32))
    hrows = prev_ref.shape[0]
    bc = lambda row: jnp.broadcast_to(row, (SUBLANES, d_a))
    xt_ref[0] = jnp.where(sub == 0, bc(prev[hrows - 2:hrows - 1, :]), pltpu.roll(xt_ref[G], 1, 0))
    xt_ref[1] = jnp.where(sub == 0, bc(prev[hrows - 1:hrows, :]), pltpu.roll(xt_ref[G + 1], 1, 0))
    xt_ref[G + 2] = jnp.where(sub == SUBLANES - 1, bc(nxt[0:1, :]), pltpu.roll(xt_ref[2], SUBLANES - 1, 0))

    cw = cw_ref[...]
    xh = (cb_ref[...][None] + cw[0:1, :][None] * xt_ref[0:G] + cw[1:2, :][None] * xt_ref[1:G + 1]
          + cw[2:3, :][None] * xt_ref[2:G + 2] + cw[3:4, :][None] * xt_ref[3:G + 3])
    xt_ref[0:G] = xh
    xhb = xh.reshape(tile, d_a).astype(BF16)
    pair = 2 * HEAD_DIM
    for j in range(d_a // pair):
        g2 = _dot(xhb[:, j * pair:(j + 1) * pair], wg_ref[j])
        gr_ref[:, :, j * pair:(j + 1) * pair] = g2[:, :pair].reshape(G, SUBLANES, pair)
        gi_ref[:, :, j * pair:(j + 1) * pair] = g2[:, pair:].reshape(G, SUBLANES, pair)

    lam = lam_ref[...]
    neg = -lam
    softplus = jnp.maximum(neg, 0.0) + jnp.log1p(jnp.exp(-jnp.abs(neg)))
    k2 = bc((-0.5 * RG_C / math.log(2.0)) * softplus)
    hba = bc(ba_ref[...])
    hbx = bc(bx_ref[...])

    def step(t, carry):
        h, acum = carry
        g = (G - 1 - t) if reverse else t
        t_r = jnp.tanh(gr_ref[g] + hba)
        t_i = jnp.tanh(gi_ref[g] + hbx)
        l2 = k2 + k2 * t_r
        a = jnp.exp2(l2)
        w = jnp.tanh(l2 * (-math.log(2.0))) * (1.0 + a * a)
        xg = xt_ref[g]
        u = jnp.sqrt(w) * (xg * t_i + xg)
        h = a * h + u
        acum = acum * a
        hl_ref[g] = h
        al_ref[g] = acum
        return h, acum

    zeros = jnp.zeros((SUBLANES, d_a), F32)
    h_end, a_end = lax.fori_loop(0, G, step, (zeros, zeros + 1.0), unroll=8)

    first = SUBLANES - 1 if reverse else 0
    shift = SUBLANES - 1 if reverse else 1
    c_in = carry_ref[...]
    c = c_in
    for _ in range(SUBLANES - 1):
        c = jnp.where(sub == first, c_in, pltpu.roll(h_end + a_end * c, shift, 0))
    out = h_end + a_end * c
    last = SUBLANES - 1 - first
    carry = bc(out[last:last + 1, :])
    carry_ref[...] = carry
    hfin_ref[...] = carry

    hfull = hl_ref[...] + al_ref[...] * c[None]
    hl_ref[...] = hfull

    def store_t(g, cc):
        for q in range(nq):
            slab_ref[q, pl.ds(g, SUBLANES, stride=pitch), :] = hl_ref[g, :, q * LANES:(q + 1) * LANES]
        return cc

    lax.fori_loop(0, G, store_t, 0, unroll=8)
    for s in range(SUBLANES):
        for q in range(nq):
            h_ref[s * G:(s + 1) * G, q * LANES:(q + 1) * LANES] = (
                slab_ref[q, s * pitch:s * pitch + G, :].astype(h_ref.dtype))


def _lru_call(p_main, conv_w_half, conv_b_half, wg, ba_half, bx_half, lam, h0, *, reverse, tile):
    b, l, _ = p_main.shape
    d_a = conv_w_half.shape[-1]
    n = l // tile
    halo = 2 * SUBLANES
    per = tile // halo
    nblk = l // halo
    G = tile // SUBLANES
    pitch = G + SUBLANES
    pos = (lambda i: n - 1 - i) if reverse else (lambda i: i)
    vec = lambda shape: pl.BlockSpec(shape, lambda bi, i: (0,) * len(shape))
    return pl.pallas_call(
        functools.partial(_lru_kernel, reverse=reverse, tile=tile),
        grid=(b, n),
        in_specs=[
            pl.BlockSpec((None, tile, d_a), lambda bi, i: (bi, pos(i), 0)),
            pl.BlockSpec((None, halo, d_a), lambda bi, i: (bi, jnp.maximum(pos(i) * per - 1, 0), 0)),
            pl.BlockSpec((None, halo, d_a), lambda bi, i: (bi, jnp.minimum((pos(i) + 1) * per, nblk - 1), 0)),
            vec((4, d_a)), vec((1, d_a)), vec(wg.shape), vec((1, d_a)), vec((1, d_a)), vec((1, d_a)),
            pl.BlockSpec((None, SUBLANES, d_a), lambda bi, i: (bi, 0, 0)),
        ],
        out_specs=[
            pl.BlockSpec((None, tile, d_a), lambda bi, i: (bi, pos(i), 0)),
            pl.BlockSpec((None, SUBLANES, d_a), lambda bi, i: (bi, 0, 0)),
        ],
        out_shape=[jax.ShapeDtypeStruct((b, l, d_a), BF16), jax.ShapeDtypeStruct((b, SUBLANES, d_a), F32)],
        scratch_shapes=[
            pltpu.VMEM((d_a // LANES, SUBLANES * pitch, LANES), F32),
            pltpu.VMEM((G + 3, SUBLANES, d_a), F32),
            pltpu.VMEM((G, SUBLANES, d_a), F32),
            pltpu.VMEM((G, SUBLANES, d_a), F32),
            pltpu.VMEM((G, SUBLANES, d_a), F32),
            pltpu.VMEM((G, SUBLANES, d_a), F32),
            pltpu.VMEM((SUBLANES, d_a), F32),
        ],
        compiler_params=_cparams("arbitrary", "arbitrary"),
        name="lru_scan_bwd" if reverse else "lru_scan_fwd",
    )(p_main, p_main, p_main, conv_w_half, conv_b_half, wg, ba_half, bx_half, lam, h0)


def _dft_direct_kernel(pq_ref, m_ref, y_ref):
    pq = pq_ref[...]
    dc = pq.shape[-1] // 2
    rhs = jnp.concatenate([pq[:, :dc], pq[:, dc:]], axis=0).astype(BF16)
    y_ref[...] = _dot(m_ref[...], rhs).astype(y_ref.dtype)


def _dft_direct_call(pq, m):
    b, l, dc2 = pq.shape
    return pl.pallas_call(
        _dft_direct_kernel,
        grid=(b,),
        in_specs=[pl.BlockSpec((None, l, dc2), lambda bi: (bi, 0, 0)),
                  pl.BlockSpec((l, 2 * l), lambda bi: (0, 0))],
        out_specs=pl.BlockSpec((None, l, dc2 // 2), lambda bi: (bi, 0, 0)),
        out_shape=jax.ShapeDtypeStruct((b, l, dc2 // 2), BF16),
        compiler_params=_cparams("parallel"),
        name="dft_direct",
    )(pq, m)


def _dft_stage1_kernel(x_ref, m_ref, ct_ref, st_ref, br_ref, bi_ref, *, dc):
    n1 = x_ref.shape[0]
    m = m_ref[...]
    ncol = x_ref.shape[1] // (2 * dc)
    for j in range(ncol):
        x = x_ref[:, j * 2 * dc:(j + 1) * 2 * dc].astype(BF16)
        cs = _dot(m, x)
        ar = cs[:n1, :dc] - cs[n1:, dc:]
        ai = cs[:n1, dc:] + cs[n1:, :dc]
        ct = ct_ref[:, j * dc:(j + 1) * dc]
        st = st_ref[:, j * dc:(j + 1) * dc]
        br_ref[:, j * dc:(j + 1) * dc] = (ct * ar - st * ai).astype(br_ref.dtype)
        bi_ref[:, j * dc:(j + 1) * dc] = (ct * ai + st * ar).astype(bi_ref.dtype)


def _dft_stage1_call(x2, m1, ct, st, dc, cols_per_step):
    b, n1, w = x2.shape
    steps = w // (cols_per_step * 2 * dc)
    return pl.pallas_call(
        functools.partial(_dft_stage1_kernel, dc=dc),
        grid=(steps, b),
        in_specs=[
            pl.BlockSpec((None, n1, cols_per_step * 2 * dc), lambda j, bi: (bi, 0, j)),
            pl.BlockSpec((2 * n1, n1), lambda j, bi: (0, 0)),
            pl.BlockSpec((n1, cols_per_step * dc), lambda j, bi: (0, j)),
            pl.BlockSpec((n1, cols_per_step * dc), lambda j, bi: (0, j)),
        ],
        out_specs=[pl.BlockSpec((None, n1, cols_per_step * dc), lambda j, bi: (bi, 0, j))] * 2,
        out_shape=[jax.ShapeDtypeStruct((b, n1, w // 2), BF16)] * 2,
        compiler_params=_cparams("parallel", "parallel"),
        name="dft_stage1",
    )(x2, m1, ct, st)


def _dft_stage2_kernel(br_ref, bi_ref, c2_ref, s2_ref, y_ref):
    c2 = c2_ref[...]
    s2 = s2_ref[...]
    for k in range(br_ref.shape[0]):
        y_ref[k] = (_dot(c2, br_ref[k].astype(BF16)) - _dot(s2, bi_ref[k].astype(BF16))).astype(y_ref.dtype)


def _dft_stage2_call(br, bi, c2, s2, k1_per_step):
    b, n1, n2, dc = br.shape
    blk = pl.BlockSpec((None, k1_per_step, n2, dc), lambda bi_, k: (bi_, k, 0, 0))
    mat = pl.BlockSpec((n2, n2), lambda bi_, k: (0, 0))
    return pl.pallas_call(
        _dft_stage2_kernel,
        grid=(b, n1 // k1_per_step),
        in_specs=[blk, blk, mat, mat],
        out_specs=blk,
        out_shape=jax.ShapeDtypeStruct((b, n1, n2, dc), BF16),
        compiler_params=_cparams("parallel", "parallel"),
        name="dft_stage2",
    )(br, bi, c2, s2)


def _dft_tables(l, scale):
    n2 = DFT_N2
    n1 = l // n2

    def cs(num, den):
        ang = (2.0 * np.pi / den) * (num % den).astype(np.float64)
        return np.cos(ang), np.sin(ang)

    k1 = np.arange(n1)
    c1, s1 = cs(np.outer(k1, k1), n1)
    l2 = np.arange(n2)
    ct, st = cs(np.outer(k1, l2), l)
    c2, s2 = cs(np.outer(l2, l2), n2)
    return c1, s1, ct, st, c2 * scale, s2 * scale


def _fourier_latent(pq, dc):
    b, l, _ = pq.shape
    n2 = DFT_N2
    n1 = l // n2
    c1, s1, ct, st, c2, s2 = _dft_tables(l, 1.0 / math.sqrt(l * HEAD_DIM))
    m1 = jnp.asarray(np.concatenate([c1, s1], axis=0), F32).astype(BF16)
    ct_e = jnp.asarray(np.repeat(ct, dc, axis=1), F32)
    st_e = jnp.asarray(np.repeat(st, dc, axis=1), F32)
    x2 = pq.reshape(b, n1, n2 * 2 * dc)
    br, bi = _dft_stage1_call(x2, m1, ct_e, st_e, dc, cols_per_step=min(8, n2))
    br = br.reshape(b, n1, n2, dc)
    bi = bi.reshape(b, n1, n2, dc)
    y = _dft_stage2_call(br, bi, jnp.asarray(c2, F32).astype(BF16), jnp.asarray(s2, F32).astype(BF16),
                         k1_per_step=min(8, n1))
    return jnp.swapaxes(y, 1, 2).reshape(b, l, dc)


def _fourier_direct(pq, dc):
    b, l, _ = pq.shape
    k = np.arange(l)
    ang = (2.0 * np.pi / l) * (np.outer(k, k) % l).astype(np.float64)
    scale = 1.0 / math.sqrt(l * HEAD_DIM)
    m = jnp.asarray(np.concatenate([np.cos(ang) * scale, -np.sin(ang) * scale], axis=1), F32).astype(BF16)
    return _dft_direct_call(pq, m)


MIX_STAGES = 5


def _mix_rows(rows, out_ref, gate_ref, uv_ref, hf_ref, hb_ref, yc_ref, x_ref, mod_ref, seg_ref, ws_ref, sgb_ref,
              gmix_ref, wout_ref, g_ref, b_ref, *, alpha, d_a, d_b):
    seg = seg_ref[...]
    inv = 1.0 / HEAD_DIM

    def seg_mean(hi, lo):
        return (_dot(hi, seg) + _dot(lo, seg)) * inv

    def rms(t, g):
        return t * lax.rsqrt(jnp.mean(t * t, axis=-1, keepdims=True) + LN_EPS) * g

    gm = gmix_ref[...]
    z = _gelu(uv_ref[rows, :].astype(F32))
    u = z[:, :d_b]
    v = z[:, d_b:]
    v_hi, v_lo = _split_bf16(v)
    yield
    dv = v - seg_mean(v_hi, v_lo)
    q_hi, q_lo = _split_bf16(dv * dv)
    ya = rms(_gelu(gate_ref[rows, :].astype(F32)) * (hf_ref[rows, :].astype(F32) + hb_ref[rows, :].astype(F32)),
             gm[:, :d_a])
    yield
    vn = dv * lax.rsqrt(seg_mean(q_hi, q_lo) + LN_EPS)
    lane = lax.broadcasted_iota(jnp.int32, (CHUNK, d_b), 1)
    heads = d_b // HEAD_DIM
    rhs = jnp.concatenate(
        [jnp.where((lane >= hh * HEAD_DIM) & (lane < (hh + 1) * HEAD_DIM), vn, 0.0) for hh in range(heads)],
        axis=0).astype(BF16)
    yc = rms(yc_ref[rows, :].astype(F32), gm[:, d_a + d_b:])
    yield
    yb = rms(u * (_dot(ws_ref[...], rhs) + sgb_ref[...]), gm[:, d_a:d_a + d_b])
    y = jnp.concatenate([ya, yb, yc], axis=-1).astype(BF16)
    yield
    o = _dot(y, wout_ref[...])
    m = mod_ref[...]
    out_ref[rows, :] = _ln(alpha * x_ref[rows, :] + m[2:3, :] * o) * g_ref[...] + b_ref[...]
    yield


def _ffn_body(x, m, wup_ref, wdn_ref, g_ref, b_ref, res_ref, acc_ref, between, *, alpha):
    res_ref[...] = x
    h = (_ln(x) * (1.0 + m[4:5, :]) + m[3:4, :]).astype(BF16)
    for j in range(wup_ref.shape[0]):
        gu = _dot(h, wup_ref[j])
        g = gu[:, :FF_CHUNK]
        act = (g * jax.nn.sigmoid(g) * gu[:, FF_CHUNK:]).astype(BF16)
        part = _dot(act, wdn_ref[j])
        if j == 0:
            acc_ref[...] = part
        else:
            acc_ref[...] += part
        between(j)
    return _ln(alpha * res_ref[...] + m[5:6, :] * acc_ref[...]) * g_ref[...] + b_ref[...]


def _mixffn_kernel(gate_ref, uv_ref, hf_ref, hb_ref, yc_ref, x_ref, modm_ref, modf_ref, seg_ref, ws_ref, sgb_ref,
                   gmix_ref, wout_ref, g1_ref, b1_ref, wup_ref, wdn_ref, g2_ref, b2_ref,
                   o_ref, x1_ref, res_ref, acc_ref, *, alpha, d_a, d_b):
    @pl.when(pl.program_id(0) == 0)
    def _():
        x1_ref[...] = jnp.zeros_like(x1_ref)

    nchunk = x_ref.shape[0] // CHUNK
    nff = wup_ref.shape[0]
    todo = []
    for k0 in range(0, nchunk, 2):
        gens = [_mix_rows(pl.ds(k * CHUNK, CHUNK), x1_ref, gate_ref, uv_ref, hf_ref, hb_ref, yc_ref, x_ref,
                          modm_ref, seg_ref, ws_ref, sgb_ref, gmix_ref, wout_ref, g1_ref, b1_ref,
                          alpha=alpha, d_a=d_a, d_b=d_b) for k in range(k0, min(k0 + 2, nchunk))]
        todo += [g for _ in range(MIX_STAGES) for g in gens]
    per_gap = -(-len(todo) // (nff - 1))

    def mix_between(j):
        count = len(todo) if j == nff - 1 else per_gap
        for _ in range(min(count, len(todo))):
            next(todo.pop(0))

    o_ref[...] = _ffn_body(x1_ref[...], modf_ref[...], wup_ref, wdn_ref, g2_ref, b2_ref, res_ref, acc_ref,
                           mix_between, alpha=alpha)


def _mixffn_call(p_main, hf, hb, yc, x, mod, seg, ws_cat, sgb, g_mix, w_out, g1, b1, w_up_r, w_dn_r, g2, b2,
                 *, alpha, tile):
    b, l, d = x.shape
    d_a = hf.shape[-1]
    d_c = yc.shape[-1]
    d_b = d_c
    n = l // tile
    total = b * n

    def mixed(j):
        def index(s):
            t = jnp.minimum(s, total - 1)
            return (t // n, t % n, j)
        return index

    def fed(s):
        t = jnp.maximum(s - 1, 0)
        return (t // n, t % n, 0)

    vec = lambda shape: pl.BlockSpec(shape, lambda s: (0,) * len(shape))
    res = lambda shape: pl.BlockSpec(shape, lambda s: (0,) * len(shape), pipeline_mode=pl.Buffered(1))
    return pl.pallas_call(
        functools.partial(_mixffn_kernel, alpha=alpha, d_a=d_a, d_b=d_b),
        grid=(total + 1,),
        in_specs=[
            pl.BlockSpec((None, tile, d_a), mixed(1)),
            pl.BlockSpec((None, tile, 2 * d_b), mixed(2)),
            pl.BlockSpec((None, tile, d_a), mixed(0)),
            pl.BlockSpec((None, tile, d_a), mixed(0)),
            pl.BlockSpec((None, tile, d_c), mixed(0)),
            pl.BlockSpec((None, tile, d), mixed(0)),
            pl.BlockSpec((None, N_MOD, d), lambda s: (jnp.minimum(s, total - 1) // n, 0, 0)),
            pl.BlockSpec((None, N_MOD, d), lambda s: (jnp.maximum(s - 1, 0) // n, 0, 0)),
            vec(seg.shape), vec(ws_cat.shape), vec(sgb.shape), vec((1, d)), vec(w_out.shape),
            vec((1, d)), vec((1, d)),
            res(w_up_r.shape), res(w_dn_r.shape), vec((1, d)), vec((1, d)),
        ],
        out_specs=pl.BlockSpec((None, tile, d), fed),
        out_shape=jax.ShapeDtypeStruct((b, l, d), F32),
        scratch_shapes=[pltpu.VMEM((tile, d), F32)] * 3,
        compiler_params=_cparams("arbitrary"),
        name="mix_ffn",
    )(p_main, p_main, hf, hb, yc, x, mod, mod, seg, ws_cat, sgb, g_mix, w_out, g1, b1, w_up_r, w_dn_r, g2, b2)


def _pos_tables(rows, dim):
    quarter = dim // 4
    freqs = POS_BASE ** (-jnp.arange(quarter, dtype=F32) / quarter)

    def enc(p):
        ang = p[:, None] * freqs[None, :]
        return jnp.concatenate([jnp.sin(ang), jnp.cos(ang)], -1)

    return enc(jnp.arange(rows, dtype=F32)), enc(jnp.arange(GRID_W, dtype=F32))


def _block_diag(blocks):
    n, r, c = blocks.shape
    eye = jnp.eye(n, dtype=blocks.dtype)
    return (eye[:, None, :, None] * blocks[:, :, None, :]).reshape(n * r, n * c)


def _gate_weights(wa, wx):
    heads = wa.shape[0]
    pa = jnp.stack([_block_diag(wa[2 * j:2 * j + 2]) for j in range(heads // 2)])
    px = jnp.stack([_block_diag(wx[2 * j:2 * j + 2]) for j in range(heads // 2)])
    return jnp.concatenate([pa, px], axis=-1).astype(BF16)


def kernel(x, c, ctx, c_ctx, w_mod, b_mod, w_in, conv_w, conv_b, lru_wa, lru_ba, lru_wx, lru_bx, lru_lam,
           sg_ws, sg_b, fourier_w, g_mix, w_out, ln1_g, ln1_b, w_up, w_down, ln2_g, ln2_b):
    bsz, seq, d = x.shape
    ctx_len = ctx.shape[1]
    depth = w_mod.shape[0]
    d_a = conv_w.shape[-1]
    d_c = fourier_w.shape[1] * fourier_w.shape[2]
    d_ff = w_down.shape[1]
    alpha = (2 * depth) ** 0.25
    lat_tile = min(512, seq)
    inproj_tile = min(2 * INPROJ_ROWS, seq)
    ctx_tile = min(256, ctx_len)

    cc = jnp.zeros((SUBLANES, d), F32).at[:bsz].set(c).at[bsz].set(c_ctx)
    mod = _mod_call(cc, w_mod, b_mod).reshape(depth, SUBLANES, N_MOD, d)

    pos = _pos_tables(seq // GRID_W, d)
    k = np.arange(HEAD_DIM)
    ang = (2.0 * np.pi / HEAD_DIM) * (np.outer(k, k) % HEAD_DIM).astype(np.float64)
    groups_c = d_c // HEAD_DIM
    cc_bd = jnp.asarray(np.kron(np.eye(groups_c), np.cos(ang)), F32)
    sc_bd = jnp.asarray(np.kron(np.eye(groups_c), np.sin(ang)), F32)
    seg = jnp.asarray(np.kron(np.eye(groups_c), np.ones((HEAD_DIM, HEAD_DIM))), BF16)
    zeros_state = jnp.zeros((bsz, SUBLANES, d_a), F32)

    xl, xc = x, ctx
    for l in range(depth):
        last = l == depth - 1
        mod_l = mod[l, :bsz]
        mod_c = jnp.broadcast_to(mod[l, bsz][None], (bsz, N_MOD, d))
        w_in_b = w_in[l].astype(BF16)
        wcs = _fold_call(cc_bd, sc_bd, _block_diag(fourier_w[l])).astype(BF16)
        cw, cb = 0.5 * conv_w[l], 0.5 * conv_b[l].reshape(1, d_a)
        lru = [(_gate_weights(lru_wa[l, dd], lru_wx[l, dd]), 0.5 * lru_ba[l, dd].reshape(1, d_a),
                0.5 * lru_bx[l, dd].reshape(1, d_a), lru_lam[l, dd].reshape(1, d_a)) for dd in range(2)]
        ws_cat = jnp.concatenate([sg_ws[l, hh] for hh in range(sg_ws.shape[1])], axis=1).astype(BF16)
        sgb = jnp.repeat(jnp.transpose(sg_b[l]), HEAD_DIM, axis=1)
        w_out_b = w_out[l].astype(BF16)
        nff = d_ff // FF_CHUNK
        w_up_r = jnp.concatenate([w_up[l][:, :d_ff].reshape(d, nff, FF_CHUNK),
                                  w_up[l][:, d_ff:].reshape(d, nff, FF_CHUNK)], axis=-1)
        w_up_r = jnp.transpose(w_up_r, (1, 0, 2)).astype(BF16)
        w_dn_r = w_down[l].reshape(nff, FF_CHUNK, d).astype(BF16)
        g1, b1 = ln1_g[l].reshape(1, d), ln1_b[l].reshape(1, d)
        g2, b2 = ln2_g[l].reshape(1, d), ln2_b[l].reshape(1, d)
        gm = g_mix[l].reshape(1, d)

        def scans(p_main, h0f, h0b, tile):
            hf, sf = _lru_call(p_main, cw, cb, *lru[0], h0f, reverse=False, tile=tile)
            hb, sb = _lru_call(p_main, cw, cb, *lru[1], h0b, reverse=True, tile=tile)
            return hf, hb, sf, sb

        def mix_ffn(p_main, hf, hb, yc, xs, mods, tile):
            return _mixffn_call(p_main, hf, hb, yc, xs, mods, seg, ws_cat, sgb, gm, w_out_b, g1, b1,
                                w_up_r, w_dn_r, g2, b2, alpha=alpha, tile=tile)

        pc_main, pc_pq = _inproj_call(xc, None, mod_c, w_in_b, wcs, ctx_tile)
        hf_c, hb_c, sf_c, sb_c = scans(pc_main, zeros_state, zeros_state, ctx_tile)
        if not last:
            xc = mix_ffn(pc_main, hf_c, hb_c, _fourier_direct(pc_pq, d_c), xc, mod_c, ctx_tile)

        if l == 0:
            xl, pl_main, pl_pq = _inproj_call(xl, pos, mod_l, w_in_b, wcs, inproj_tile)
        else:
            pl_main, pl_pq = _inproj_call(xl, None, mod_l, w_in_b, wcs, inproj_tile)
        hf, hb, _, _ = scans(pl_main, sf_c, sb_c, lat_tile)
        if seq > 2 * DFT_N2:
            yc = _fourier_latent(pl_pq, d_c)
        else:
            yc = _fourier_direct(pl_pq, d_c)
        xl = mix_ffn(pl_main, hf, hb, yc, xl, mod_l, lat_tile)
    return xl
```

```python
import functools
import math

import jax
import jax.numpy as jnp
import numpy as np
from jax import lax
from jax.experimental import pallas as pl
from jax.experimental.pallas import tpu as pltpu

HEAD_DIM = 64
CHUNK = 128
GRID_W = 64
RG_C = 8.0
LN_EPS = 1e-6
POS_BASE = 10000.0
N_MOD = 6
SUBLANES = 8
LANES = 128
HALO = 2 * SUBLANES
FF_CHUNK = 256
DFT_N2 = 128
INPROJ_ROWS = 512
VMEM_LIMIT = 56 * 1024 * 1024

F32 = jnp.float32
BF16 = jnp.bfloat16


def _dot(a, b):
    return jnp.dot(a, b, preferred_element_type=F32)


def _ln(x):
    mu = jnp.mean(x, axis=-1, keepdims=True)
    d = x - mu
    var = jnp.mean(d * d, axis=-1, keepdims=True)
    return d * lax.rsqrt(var + LN_EPS)


def _gelu(x):
    return 0.5 * x * (1.0 + lax.erf(x * (1.0 / math.sqrt(2.0))))


def _cparams(*sem):
    return pltpu.CompilerParams(dimension_semantics=sem, vmem_limit_bytes=VMEM_LIMIT)


def _split_bf16(t):
    hi = t.astype(BF16)
    return hi, (t - hi.astype(F32)).astype(BF16)


def _mod_kernel(c_ref, w_ref, b_ref, o_ref):
    c = c_ref[...]
    s = (c * jax.nn.sigmoid(c)).astype(BF16)
    o_ref[...] = _dot(s, w_ref[...].astype(BF16)) + b_ref[...]


def _mod_call(cc, w_mod, b_mod):
    depth, d, nm = w_mod.shape
    tn = 1536
    return pl.pallas_call(
        _mod_kernel,
        grid=(depth, nm // tn),
        in_specs=[
            pl.BlockSpec((SUBLANES, d), lambda l, j: (0, 0)),
            pl.BlockSpec((None, d, tn), lambda l, j: (l, 0, j)),
            pl.BlockSpec((None, 1, tn), lambda l, j: (l, 0, j)),
        ],
        out_specs=pl.BlockSpec((None, SUBLANES, tn), lambda l, j: (l, 0, j)),
        out_shape=jax.ShapeDtypeStruct((depth, SUBLANES, nm), F32),
        compiler_params=_cparams("parallel", "parallel"),
        name="mod_vectors",
    )(cc, w_mod, b_mod.reshape(depth, 1, nm))


def _dot_3pass(a, b):
    a_hi, a_lo = _split_bf16(a)
    b_hi, b_lo = _split_bf16(b)
    return _dot(a_hi, b_hi) + (_dot(a_hi, b_lo) + _dot(a_lo, b_hi))


def _fold_kernel(cc_ref, sc_ref, wf_ref, o_ref):
    wf = wf_ref[...]
    n = wf.shape[0]
    o_ref[:, :n] = _dot_3pass(cc_ref[...], wf)
    o_ref[:, n:] = _dot_3pass(sc_ref[...], wf)


def _fold_call(cc_bd, sc_bd, wf_bd):
    n = wf_bd.shape[0]
    return pl.pallas_call(
        _fold_kernel,
        out_shape=jax.ShapeDtypeStruct((n, 2 * n), F32),
        name="fourier_weight_fold",
    )(cc_bd, sc_bd, wf_bd)


def _inproj_kernel(*refs, add_pos, d_main):
    if add_pos:
        x_ref, prow_ref, pcol_ref, mod_ref, w_ref, wcs_ref, xo_ref, pm_ref, pq_ref = refs
    else:
        x_ref, mod_ref, w_ref, wcs_ref, pm_ref, pq_ref = refs
    tile, d = x_ref.shape
    m = mod_ref[...]
    sub = min(tile, INPROJ_ROWS)

    def normed(k):
        rows = pl.ds(k * sub, sub)
        if add_pos:
            half = d // 2
            nrow = sub // GRID_W
            prow = prow_ref[pl.ds(k * nrow, nrow), :]
            xr = x_ref[rows, :half].reshape(nrow, GRID_W, half) + prow[:, None, :]
            xq = x_ref[rows, half:].reshape(nrow, GRID_W, half) + pcol_ref[...][None, :, :]
            x = jnp.concatenate([xr.reshape(sub, half), xq.reshape(sub, half)], axis=-1)
            xo_ref[rows, :] = x
        else:
            x = x_ref[rows, :]
        return (_ln(x) * (1.0 + m[1:2, :]) + m[0:1, :]).astype(BF16)

    nsub = tile // sub
    h = normed(0)
    for k in range(nsub):
        rows = pl.ds(k * sub, sub)
        p = _dot(h, w_ref[...])
        if k + 1 < nsub:
            h = normed(k + 1)
        pm_ref[rows, :] = p[:, :d_main].astype(pm_ref.dtype)
        pq_ref[rows, :] = _dot(p[:, d_main:].astype(BF16), wcs_ref[...]).astype(pq_ref.dtype)


def _inproj_call(x, pos, mod, w_in, wcs, tile):
    b, l, d = x.shape
    d_in = w_in.shape[1]
    d_c = wcs.shape[0]
    d_main = d_in - d_c
    add_pos = pos is not None
    tok = lambda bi, i: (bi, i, 0)
    in_specs = [pl.BlockSpec((None, tile, d), tok)]
    args = [x]
    if add_pos:
        prow, pcol = pos
        in_specs.append(pl.BlockSpec((tile // GRID_W, d // 2), lambda bi, i: (i, 0)))
        in_specs.append(pl.BlockSpec((GRID_W, d // 2), lambda bi, i: (0, 0)))
        args += [prow, pcol]
    in_specs += [
        pl.BlockSpec((None, N_MOD, d), lambda bi, i: (bi, 0, 0)),
        pl.BlockSpec((d, d_in), lambda bi, i: (0, 0)),
        pl.BlockSpec((d_c, 2 * d_c), lambda bi, i: (0, 0)),
    ]
    args += [mod, w_in, wcs]
    out_specs = [pl.BlockSpec((None, tile, d_main), tok), pl.BlockSpec((None, tile, 2 * d_c), tok)]
    out_shape = [jax.ShapeDtypeStruct((b, l, d_main), BF16), jax.ShapeDtypeStruct((b, l, 2 * d_c), BF16)]
    if add_pos:
        out_specs.insert(0, pl.BlockSpec((None, tile, d), tok))
        out_shape.insert(0, jax.ShapeDtypeStruct((b, l, d), F32))
    return pl.pallas_call(
        functools.partial(_inproj_kernel, add_pos=add_pos, d_main=d_main),
        grid=(b, l // tile),
        in_specs=in_specs,
        out_specs=out_specs,
        out_shape=out_shape,
        compiler_params=_cparams("parallel", "parallel"),
        name="in_proj",
    )(*args)


def _scan_scratch(tile, d_a):
    G = tile // SUBLANES
    pitch = G + SUBLANES
    return [
        pltpu.VMEM((d_a // LANES, SUBLANES * pitch, LANES), F32),
        pltpu.VMEM((G + 3, SUBLANES, d_a), F32),
        pltpu.VMEM((G, SUBLANES, d_a), F32),
        pltpu.VMEM((G, SUBLANES, d_a), F32),
        pltpu.VMEM((G, SUBLANES, d_a), F32),
        pltpu.VMEM((G, SUBLANES, d_a), F32),
        pltpu.VMEM((SUBLANES, d_a), F32),
    ]


def _scan_tile(x_ref, prev_ref, next_ref, cw_ref, cb_ref, wg_ref, ba_ref, bx_ref, lam_ref,
               out_ref, slab_ref, xt_ref, gr_ref, gi_ref, hl_ref, al_ref, carry_ref,
               *, reverse, seq_first, seq_last):
    tile, d_a = x_ref.shape
    G = tile // SUBLANES
    nq = d_a // LANES
    pitch = slab_ref.shape[1] // SUBLANES
    sub = lax.broadcasted_iota(jnp.int32, (SUBLANES, d_a), 0)
    bc = lambda row: jnp.broadcast_to(row, (SUBLANES, d_a))

    xv = x_ref[...].astype(F32)
    for s in range(SUBLANES):
        for q in range(nq):
            slab_ref[q, s * pitch:s * pitch + G, :] = xv[s * G:(s + 1) * G, q * LANES:(q + 1) * LANES]

    for g in range(G):
        for q in range(nq):
            xt_ref[g + 2, :, q * LANES:(q + 1) * LANES] = slab_ref[q, pl.ds(g, SUBLANES, stride=pitch), :]
    prev = jnp.where(seq_first, 0.0, prev_ref[...].astype(F32))
    nxt = jnp.where(seq_last, 0.0, next_ref[...].astype(F32))
    hrows = prev_ref.shape[0]
    xt_ref[0] = jnp.where(sub == 0, bc(prev[hrows - 2:hrows - 1, :]), pltpu.roll(xt_ref[G], 1, 0))
    xt_ref[1] = jnp.where(sub == 0, bc(prev[hrows - 1:hrows, :]), pltpu.roll(xt_ref[G + 1], 1, 0))
    xt_ref[G + 2] = jnp.where(sub == SUBLANES - 1, bc(nxt[0:1, :]), pltpu.roll(xt_ref[2], SUBLANES - 1, 0))
    cw = cw_ref[...]
    xh = (cb_ref[...][None] + cw[0:1, :][None] * xt_ref[0:G] + cw[1:2, :][None] * xt_ref[1:G + 1]
          + cw[2:3, :][None] * xt_ref[2:G + 2] + cw[3:4, :][None] * xt_ref[3:G + 3])
    xt_ref[0:G] = xh
    xhb = xh.reshape(tile, d_a).astype(BF16)

    pair = 2 * HEAD_DIM
    for j in range(d_a // pair):
        g2 = _dot(xhb[:, j * pair:(j + 1) * pair], wg_ref[j])
        gr_ref[:, :, j * pair:(j + 1) * pair] = g2[:, :pair].reshape(G, SUBLANES, pair)
        gi_ref[:, :, j * pair:(j + 1) * pair] = g2[:, pair:].reshape(G, SUBLANES, pair)
    lam = lam_ref[...]
    neg = -lam
    softplus = jnp.maximum(neg, 0.0) + jnp.log1p(jnp.exp(-jnp.abs(neg)))
    k2 = bc((-0.5 * RG_C / math.log(2.0)) * softplus)
    hba = bc(ba_ref[...])
    hbx = bc(bx_ref[...])

    h = jnp.zeros((SUBLANES, d_a), F32)
    acum = h + 1.0
    for t in range(G):
        g = (G - 1 - t) if reverse else t
        t_r = jnp.tanh(gr_ref[g] + hba)
        t_i = jnp.tanh(gi_ref[g] + hbx)
        l2 = k2 + k2 * t_r
        a = jnp.exp2(l2)
        w = jnp.tanh(l2 * (-math.log(2.0))) * (1.0 + a * a)
        xg = xt_ref[g]
        h = a * h + jnp.sqrt(w) * (xg * t_i + xg)
        acum = acum * a
        hl_ref[g] = h
        al_ref[g] = acum

    first = SUBLANES - 1 if reverse else 0
    shift = SUBLANES - 1 if reverse else 1
    c_in = carry_ref[...]
    c = c_in
    for _ in range(SUBLANES - 1):
        c = jnp.where(sub == first, c_in, pltpu.roll(h + acum * c, shift, 0))
    last = SUBLANES - 1 - first
    carry_ref[...] = bc((h + acum * c)[last:last + 1, :])

    for g in range(G):
        hfull = hl_ref[g] + al_ref[g] * c
        for q in range(nq):
            slab_ref[q, pl.ds(g, SUBLANES, stride=pitch), :] = hfull[:, q * LANES:(q + 1) * LANES]
    for s in range(SUBLANES):
        for q in range(nq):
            out_ref[s * G:(s + 1) * G, q * LANES:(q + 1) * LANES] = (
                slab_ref[q, s * pitch:s * pitch + G, :].astype(out_ref.dtype))


def _lru_kernel(x_ref, prev_ref, next_ref, cw_ref, cb_ref, wg_ref, ba_ref, bx_ref, lam_ref, h0_ref,
                h_ref, hfin_ref, *scratch, reverse):
    i = pl.program_id(1)
    n = pl.num_programs(1)
    li = (n - 1 - i) if reverse else i

    @pl.when(i == 0)
    def _():
        scratch[-1][...] = h0_ref[...]

    _scan_tile(x_ref, prev_ref, next_ref, cw_ref, cb_ref, wg_ref, ba_ref, bx_ref, lam_ref,
               h_ref, *scratch, reverse=reverse, seq_first=li == 0, seq_last=li == n - 1)
    hfin_ref[...] = scratch[-1][...]


def _lru_call(p_main, conv_w_half, conv_b_half, wg, ba_half, bx_half, lam, h0, *, reverse, tile):
    b, l, _ = p_main.shape
    d_a = conv_w_half.shape[-1]
    n = l // tile
    per = tile // HALO
    nblk = l // HALO
    pos = (lambda i: n - 1 - i) if reverse else (lambda i: i)
    vec = lambda shape: pl.BlockSpec(shape, lambda bi, i: (0,) * len(shape))
    return pl.pallas_call(
        functools.partial(_lru_kernel, reverse=reverse),
        grid=(b, n),
        in_specs=[
            pl.BlockSpec((None, tile, d_a), lambda bi, i: (bi, pos(i), 0)),
            pl.BlockSpec((None, HALO, d_a), lambda bi, i: (bi, jnp.maximum(pos(i) * per - 1, 0), 0)),
            pl.BlockSpec((None, HALO, d_a), lambda bi, i: (bi, jnp.minimum((pos(i) + 1) * per, nblk - 1), 0)),
            vec((4, d_a)), vec((1, d_a)), vec(wg.shape), vec((1, d_a)), vec((1, d_a)), vec((1, d_a)),
            pl.BlockSpec((None, SUBLANES, d_a), lambda bi, i: (bi, 0, 0)),
        ],
        out_specs=[
            pl.BlockSpec((None, tile, d_a), lambda bi, i: (bi, pos(i), 0)),
            pl.BlockSpec((None, SUBLANES, d_a), lambda bi, i: (bi, 0, 0)),
        ],
        out_shape=[jax.ShapeDtypeStruct((b, l, d_a), BF16), jax.ShapeDtypeStruct((b, SUBLANES, d_a), F32)],
        scratch_shapes=_scan_scratch(tile, d_a),
        compiler_params=_cparams("arbitrary", "arbitrary"),
        name="lru_scan_bwd" if reverse else "lru_scan_fwd",
    )(p_main, p_main, p_main, conv_w_half, conv_b_half, wg, ba_half, bx_half, lam, h0)


def _dft_direct_kernel(pq_ref, m_ref, y_ref):
    pq = pq_ref[...]
    dc = pq.shape[-1] // 2
    rhs = jnp.concatenate([pq[:, :dc], pq[:, dc:]], axis=0).astype(BF16)
    y_ref[...] = _dot(m_ref[...], rhs).astype(y_ref.dtype)


def _dft_direct_call(pq, m):
    b, l, dc2 = pq.shape
    return pl.pallas_call(
        _dft_direct_kernel,
        grid=(b,),
        in_specs=[pl.BlockSpec((None, l, dc2), lambda bi: (bi, 0, 0)),
                  pl.BlockSpec((l, 2 * l), lambda bi: (0, 0))],
        out_specs=pl.BlockSpec((None, l, dc2 // 2), lambda bi: (bi, 0, 0)),
        out_shape=jax.ShapeDtypeStruct((b, l, dc2 // 2), BF16),
        compiler_params=_cparams("parallel"),
        name="dft_direct",
    )(pq, m)


def _dft_stage1_kernel(x_ref, m_ref, ct_ref, st_ref, br_ref, bi_ref, *, dc):
    n1 = x_ref.shape[0]
    m = m_ref[...]
    ncol = x_ref.shape[1] // (2 * dc)
    for j in range(ncol):
        x = x_ref[:, j * 2 * dc:(j + 1) * 2 * dc].astype(BF16)
        cs = _dot(m, x)
        ar = cs[:n1, :dc] - cs[n1:, dc:]
        ai = cs[:n1, dc:] + cs[n1:, :dc]
        ct = ct_ref[:, j * dc:(j + 1) * dc]
        st = st_ref[:, j * dc:(j + 1) * dc]
        br_ref[:, j * dc:(j + 1) * dc] = (ct * ar - st * ai).astype(br_ref.dtype)
        bi_ref[:, j * dc:(j + 1) * dc] = (ct * ai + st * ar).astype(bi_ref.dtype)


def _dft_stage1_call(x2, m1, ct, st, dc, cols_per_step):
    b, n1, w = x2.shape
    steps = w // (cols_per_step * 2 * dc)
    return pl.pallas_call(
        functools.partial(_dft_stage1_kernel, dc=dc),
        grid=(steps, b),
        in_specs=[
            pl.BlockSpec((None, n1, cols_per_step * 2 * dc), lambda j, bi: (bi, 0, j)),
            pl.BlockSpec((2 * n1, n1), lambda j, bi: (0, 0)),
            pl.BlockSpec((n1, cols_per_step * dc), lambda j, bi: (0, j)),
            pl.BlockSpec((n1, cols_per_step * dc), lambda j, bi: (0, j)),
        ],
        out_specs=[pl.BlockSpec((None, n1, cols_per_step * dc), lambda j, bi: (bi, 0, j))] * 2,
        out_shape=[jax.ShapeDtypeStruct((b, n1, w // 2), BF16)] * 2,
        compiler_params=_cparams("parallel", "parallel"),
        name="dft_stage1",
    )(x2, m1, ct, st)


def _dft_stage2_kernel(br_ref, bi_ref, c2_ref, s2_ref, y_ref):
    c2 = c2_ref[...]
    s2 = s2_ref[...]
    for k in range(br_ref.shape[0]):
        y_ref[k] = (_dot(c2, br_ref[k].astype(BF16)) - _dot(s2, bi_ref[k].astype(BF16))).astype(y_ref.dtype)


def _dft_stage2_call(br, bi, c2, s2, k1_per_step):
    b, n1, n2, dc = br.shape
    blk = pl.BlockSpec((None, k1_per_step, n2, dc), lambda bi_, k: (bi_, k, 0, 0))
    mat = pl.BlockSpec((n2, n2), lambda bi_, k: (0, 0))
    return pl.pallas_call(
        _dft_stage2_kernel,
        grid=(b, n1 // k1_per_step),
        in_specs=[blk, blk, mat, mat],
        out_specs=blk,
        out_shape=jax.ShapeDtypeStruct((b, n1, n2, dc), BF16),
        compiler_params=_cparams("parallel", "parallel"),
        name="dft_stage2",
    )(br, bi, c2, s2)


def _dft_tables(l, scale):
    n2 = DFT_N2
    n1 = l // n2

    def cs(num, den):
        ang = (2.0 * np.pi / den) * (num % den).astype(np.float64)
        return np.cos(ang), np.sin(ang)

    k1 = np.arange(n1)
    c1, s1 = cs(np.outer(k1, k1), n1)
    l2 = np.arange(n2)
    ct, st = cs(np.outer(k1, l2), l)
    c2, s2 = cs(np.outer(l2, l2), n2)
    return c1, s1, ct, st, c2 * scale, s2 * scale


def _fourier_latent(pq, dc):
    b, l, _ = pq.shape
    n2 = DFT_N2
    n1 = l // n2
    c1, s1, ct, st, c2, s2 = _dft_tables(l, 1.0 / math.sqrt(l * HEAD_DIM))
    m1 = jnp.asarray(np.concatenate([c1, s1], axis=0), F32).astype(BF16)
    ct_e = jnp.asarray(np.repeat(ct, dc, axis=1), F32)
    st_e = jnp.asarray(np.repeat(st, dc, axis=1), F32)
    x2 = pq.reshape(b, n1, n2 * 2 * dc)
    br, bi = _dft_stage1_call(x2, m1, ct_e, st_e, dc, cols_per_step=min(8, n2))
    br = br.reshape(b, n1, n2, dc)
    bi = bi.reshape(b, n1, n2, dc)
    y = _dft_stage2_call(br, bi, jnp.asarray(c2, F32).astype(BF16), jnp.asarray(s2, F32).astype(BF16),
                         k1_per_step=min(8, n1))
    return jnp.swapaxes(y, 1, 2).reshape(b, l, dc)


def _fourier_direct(pq, dc):
    b, l, _ = pq.shape
    k = np.arange(l)
    ang = (2.0 * np.pi / l) * (np.outer(k, k) % l).astype(np.float64)
    scale = 1.0 / math.sqrt(l * HEAD_DIM)
    m = jnp.asarray(np.concatenate([np.cos(ang) * scale, -np.sin(ang) * scale], axis=1), F32).astype(BF16)
    return _dft_direct_call(pq, m)


MIX_STAGES = 5


def _mix_rows(rows, out_ref, gate_ref, uv_ref, hf_ref, hb_ref, yc_ref, x_ref, mod_ref, seg_ref, ws_ref, sgb_ref,
              gmix_ref, wout_ref, g_ref, b_ref, *, alpha, d_a, d_b):
    seg = seg_ref[...]
    inv = 1.0 / HEAD_DIM

    def seg_mean(hi, lo):
        return (_dot(hi, seg) + _dot(lo, seg)) * inv

    def rms(t, g):
        return t * lax.rsqrt(jnp.mean(t * t, axis=-1, keepdims=True) + LN_EPS) * g

    gm = gmix_ref[...]
    z = _gelu(uv_ref[rows, :].astype(F32))
    u = z[:, :d_b]
    v = z[:, d_b:]
    v_hi, v_lo = _split_bf16(v)
    yield
    dv = v - seg_mean(v_hi, v_lo)
    q_hi, q_lo = _split_bf16(dv * dv)
    ya = rms(_gelu(gate_ref[rows, :].astype(F32)) * (hf_ref[rows, :].astype(F32) + hb_ref[rows, :].astype(F32)),
             gm[:, :d_a])
    yield
    vn = dv * lax.rsqrt(seg_mean(q_hi, q_lo) + LN_EPS)
    lane = lax.broadcasted_iota(jnp.int32, (CHUNK, d_b), 1)
    heads = d_b // HEAD_DIM
    rhs = jnp.concatenate(
        [jnp.where((lane >= hh * HEAD_DIM) & (lane < (hh + 1) * HEAD_DIM), vn, 0.0) for hh in range(heads)],
        axis=0).astype(BF16)
    yc = rms(yc_ref[rows, :].astype(F32), gm[:, d_a + d_b:])
    yield
    yb = rms(u * (_dot(ws_ref[...], rhs) + sgb_ref[...]), gm[:, d_a:d_a + d_b])
    y = jnp.concatenate([ya, yb, yc], axis=-1).astype(BF16)
    yield
    o = _dot(y, wout_ref[...])
    m = mod_ref[...]
    out_ref[rows, :] = _ln(alpha * x_ref[rows, :] + m[2:3, :] * o) * g_ref[...] + b_ref[...]
    yield


def _ffn_body(x, m, wup_ref, wdn_ref, g_ref, b_ref, res_ref, between, *, alpha):
    res_ref[...] = x
    h = (_ln(x) * (1.0 + m[4:5, :]) + m[3:4, :]).astype(BF16)
    for j in range(wup_ref.shape[0]):
        gu = _dot(h, wup_ref[j])
        g = gu[:, :FF_CHUNK]
        act = (g * jax.nn.sigmoid(g) * gu[:, FF_CHUNK:]).astype(BF16)
        part = _dot(act, wdn_ref[j])
        acc = part if j == 0 else acc + part
        between(j)
    return _ln(alpha * res_ref[...] + m[5:6, :] * acc) * g_ref[...] + b_ref[...]


def _mixffn_kernel(gate_ref, uv_ref, hf_ref, hb_ref, yc_ref, x_ref, modm_ref, modf_ref, seg_ref, ws_ref, sgb_ref,
                   gmix_ref, wout_ref, g1_ref, b1_ref, wup_ref, wdn_ref, g2_ref, b2_ref,
                   o_ref, x1_ref, res_ref, *, alpha, d_a, d_b):
    @pl.when(pl.program_id(0) == 0)
    def _():
        x1_ref[...] = jnp.zeros_like(x1_ref)

    nchunk = x_ref.shape[0] // CHUNK
    nff = wup_ref.shape[0]
    todo = []
    for k0 in range(0, nchunk, 2):
        gens = [_mix_rows(pl.ds(c * CHUNK, CHUNK), x1_ref, gate_ref, uv_ref, hf_ref, hb_ref, yc_ref, x_ref,
                          modm_ref, seg_ref, ws_ref, sgb_ref, gmix_ref, wout_ref, g1_ref, b1_ref,
                          alpha=alpha, d_a=d_a, d_b=d_b) for c in range(k0, min(k0 + 2, nchunk))]
        todo += [g for _ in range(MIX_STAGES) for g in gens]
    per_gap = -(-len(todo) // (nff - 1))

    def between(j):
        count = len(todo) if j == nff - 1 else per_gap
        for _ in range(min(count, len(todo))):
            next(todo.pop(0))

    o_ref[...] = _ffn_body(x1_ref[...], modf_ref[...], wup_ref, wdn_ref, g2_ref, b2_ref, res_ref, between,
                           alpha=alpha)


def _mixffn_call(p_main, hf, hb, yc, x, mod, seg, ws_cat, sgb, g_mix, w_out, g1, b1, w_up_r, w_dn_r, g2, b2,
                 *, alpha, tile):
    b, l, d = x.shape
    d_a = hf.shape[-1]
    d_c = yc.shape[-1]
    d_b = d_c
    n = l // tile
    total = b * n

    def mixed(j):
        def index(s):
            t = jnp.minimum(s, total - 1)
            return (t // n, t % n, j)
        return index

    def fed(s):
        t = jnp.maximum(s - 1, 0)
        return (t // n, t % n, 0)

    vec = lambda shape: pl.BlockSpec(shape, lambda s: (0,) * len(shape))
    res = lambda shape: pl.BlockSpec(shape, lambda s: (0,) * len(shape), pipeline_mode=pl.Buffered(1))
    return pl.pallas_call(
        functools.partial(_mixffn_kernel, alpha=alpha, d_a=d_a, d_b=d_b),
        grid=(total + 1,),
        in_specs=[
            pl.BlockSpec((None, tile, d_a), mixed(1)),
            pl.BlockSpec((None, tile, 2 * d_b), mixed(2)),
            pl.BlockSpec((None, tile, d_a), mixed(0)),
            pl.BlockSpec((None, tile, d_a), mixed(0)),
            pl.BlockSpec((None, tile, d_c), mixed(0)),
            pl.BlockSpec((None, tile, d), mixed(0)),
            pl.BlockSpec((None, N_MOD, d), lambda s: (jnp.minimum(s, total - 1) // n, 0, 0)),
            pl.BlockSpec((None, N_MOD, d), lambda s: (jnp.maximum(s - 1, 0) // n, 0, 0)),
            vec(seg.shape), vec(ws_cat.shape), vec(sgb.shape), vec((1, d)), vec(w_out.shape),
            vec((1, d)), vec((1, d)),
            res(w_up_r.shape), res(w_dn_r.shape), vec((1, d)), vec((1, d)),
        ],
        out_specs=pl.BlockSpec((None, tile, d), fed),
        out_shape=jax.ShapeDtypeStruct((b, l, d), F32),
        scratch_shapes=[pltpu.VMEM((tile, d), F32)] * 2,
        compiler_params=_cparams("arbitrary"),
        name="mix_ffn",
    )(p_main, p_main, hf, hb, yc, x, mod, mod, seg, ws_cat, sgb, g_mix, w_out, g1, b1, w_up_r, w_dn_r, g2, b2)


def _pos_tables(rows, dim):
    quarter = dim // 4
    freqs = POS_BASE ** (-jnp.arange(quarter, dtype=F32) / quarter)

    def enc(p):
        ang = p[:, None] * freqs[None, :]
        return jnp.concatenate([jnp.sin(ang), jnp.cos(ang)], -1)

    return enc(jnp.arange(rows, dtype=F32)), enc(jnp.arange(GRID_W, dtype=F32))


def _block_diag(blocks):
    n, r, c = blocks.shape
    eye = jnp.eye(n, dtype=blocks.dtype)
    return (eye[:, None, :, None] * blocks[:, :, None, :]).reshape(n * r, n * c)


def _gate_weights(wa, wx):
    heads = wa.shape[0]
    pa = jnp.stack([_block_diag(wa[2 * j:2 * j + 2]) for j in range(heads // 2)])
    px = jnp.stack([_block_diag(wx[2 * j:2 * j + 2]) for j in range(heads // 2)])
    return jnp.concatenate([pa, px], axis=-1).astype(BF16)


def kernel(x, c, ctx, c_ctx, w_mod, b_mod, w_in, conv_w, conv_b, lru_wa, lru_ba, lru_wx, lru_bx, lru_lam,
           sg_ws, sg_b, fourier_w, g_mix, w_out, ln1_g, ln1_b, w_up, w_down, ln2_g, ln2_b):
    bsz, seq, d = x.shape
    ctx_len = ctx.shape[1]
    depth = w_mod.shape[0]
    d_a = conv_w.shape[-1]
    d_c = fourier_w.shape[1] * fourier_w.shape[2]
    d_ff = w_down.shape[1]
    alpha = (2 * depth) ** 0.25
    lat_tile = min(512, seq)
    inproj_tile = min(2 * INPROJ_ROWS, seq)
    ctx_tile = min(256, ctx_len)

    cc = jnp.zeros((SUBLANES, d), F32).at[:bsz].set(c).at[bsz].set(c_ctx)
    mod = _mod_call(cc, w_mod, b_mod).reshape(depth, SUBLANES, N_MOD, d)

    pos = _pos_tables(seq // GRID_W, d)
    k = np.arange(HEAD_DIM)
    ang = (2.0 * np.pi / HEAD_DIM) * (np.outer(k, k) % HEAD_DIM).astype(np.float64)
    groups_c = d_c // HEAD_DIM
    cc_bd = jnp.asarray(np.kron(np.eye(groups_c), np.cos(ang)), F32)
    sc_bd = jnp.asarray(np.kron(np.eye(groups_c), np.sin(ang)), F32)
    seg = jnp.asarray(np.kron(np.eye(groups_c), np.ones((HEAD_DIM, HEAD_DIM))), BF16)
    zeros_state = jnp.zeros((bsz, SUBLANES, d_a), F32)

    xl, xc = x, ctx
    for l in range(depth):
        last = l == depth - 1
        mod_l = mod[l, :bsz]
        mod_c = jnp.broadcast_to(mod[l, bsz][None], (bsz, N_MOD, d))
        w_in_b = w_in[l].astype(BF16)
        wcs = _fold_call(cc_bd, sc_bd, _block_diag(fourier_w[l])).astype(BF16)
        cw, cb = 0.5 * conv_w[l], 0.5 * conv_b[l].reshape(1, d_a)
        lru = [(cw, cb, _gate_weights(lru_wa[l, dd], lru_wx[l, dd]), 0.5 * lru_ba[l, dd].reshape(1, d_a),
                0.5 * lru_bx[l, dd].reshape(1, d_a), lru_lam[l, dd].reshape(1, d_a)) for dd in range(2)]
        ws_cat = jnp.concatenate([sg_ws[l, hh] for hh in range(sg_ws.shape[1])], axis=1).astype(BF16)
        sgb = jnp.repeat(jnp.transpose(sg_b[l]), HEAD_DIM, axis=1)
        w_out_b = w_out[l].astype(BF16)
        nff = d_ff // FF_CHUNK
        w_up_r = jnp.concatenate([w_up[l][:, :d_ff].reshape(d, nff, FF_CHUNK),
                                  w_up[l][:, d_ff:].reshape(d, nff, FF_CHUNK)], axis=-1)
        w_up_r = jnp.transpose(w_up_r, (1, 0, 2)).astype(BF16)
        w_dn_r = w_down[l].reshape(nff, FF_CHUNK, d).astype(BF16)
        g1, b1 = ln1_g[l].reshape(1, d), ln1_b[l].reshape(1, d)
        g2, b2 = ln2_g[l].reshape(1, d), ln2_b[l].reshape(1, d)
        gm = g_mix[l].reshape(1, d)

        def mix_ffn(p_main, hf, hb, yc, xs, mods, tile):
            return _mixffn_call(p_main, hf, hb, yc, xs, mods, seg, ws_cat, sgb, gm, w_out_b, g1, b1,
                                w_up_r, w_dn_r, g2, b2, alpha=alpha, tile=tile)

        pc_main, pc_pq = _inproj_call(xc, None, mod_c, w_in_b, wcs, ctx_tile)
        hf_c, sf_c = _lru_call(pc_main, *lru[0], zeros_state, reverse=False, tile=ctx_tile)
        hb_c, sb_c = _lru_call(pc_main, *lru[1], zeros_state, reverse=True, tile=ctx_tile)
        if not last:
            xc = mix_ffn(pc_main, hf_c, hb_c, _fourier_direct(pc_pq, d_c), xc, mod_c, ctx_tile)

        if l == 0:
            xl, pl_main, pl_pq = _inproj_call(xl, pos, mod_l, w_in_b, wcs, inproj_tile)
        else:
            pl_main, pl_pq = _inproj_call(xl, None, mod_l, w_in_b, wcs, inproj_tile)
        hf, _ = _lru_call(pl_main, *lru[0], sf_c, reverse=False, tile=lat_tile)
        hb, _ = _lru_call(pl_main, *lru[1], sb_c, reverse=True, tile=lat_tile)
        if seq > 2 * DFT_N2:
            yc = _fourier_latent(pl_pq, d_c)
        else:
            yc = _fourier_direct(pl_pq, d_c)
        xl = mix_ffn(pl_main, hf, hb, yc, xl, mod_l, lat_tile)
    return xl
```

```python
import functools
import math

import jax
import jax.numpy as jnp
import numpy as np
from jax import lax
from jax.experimental import pallas as pl
from jax.experimental.pallas import tpu as pltpu

HEAD_DIM = 64
CHUNK = 128
GRID_W = 64
RG_C = 8.0
LN_EPS = 1e-6
POS_BASE = 10000.0
N_MOD = 6
SUBLANES = 8
LANES = 128
HALO = 2 * SUBLANES
FF_CHUNK = 256
DFT_N2 = 128
DFT_TILE = HALO
DFT_PITCH = DFT_TILE + SUBLANES
INPROJ_ROWS = 512
VMEM_LIMIT = 56 * 1024 * 1024

F32 = jnp.float32
BF16 = jnp.bfloat16


def _dot(a, b):
    return jnp.dot(a, b, preferred_element_type=F32)


def _ln(x):
    mu = jnp.mean(x, axis=-1, keepdims=True)
    d = x - mu
    var = jnp.mean(d * d, axis=-1, keepdims=True)
    return d * lax.rsqrt(var + LN_EPS)


def _gelu(x):
    return 0.5 * x * (1.0 + lax.erf(x * (1.0 / math.sqrt(2.0))))


def _cparams(*sem):
    return pltpu.CompilerParams(dimension_semantics=sem, vmem_limit_bytes=VMEM_LIMIT)


def _split_bf16(t):
    hi = t.astype(BF16)
    return hi, (t - hi.astype(F32)).astype(BF16)


def _mod_kernel(c_ref, w_ref, b_ref, o_ref):
    c = c_ref[...]
    s = (c * jax.nn.sigmoid(c)).astype(BF16)
    o_ref[...] = _dot(s, w_ref[...].astype(BF16)) + b_ref[...]


def _mod_call(cc, w_mod, b_mod):
    depth, d, nm = w_mod.shape
    tn = 1536
    return pl.pallas_call(
        _mod_kernel,
        grid=(depth, nm // tn),
        in_specs=[
            pl.BlockSpec((SUBLANES, d), lambda l, j: (0, 0)),
            pl.BlockSpec((None, d, tn), lambda l, j: (l, 0, j)),
            pl.BlockSpec((None, 1, tn), lambda l, j: (l, 0, j)),
        ],
        out_specs=pl.BlockSpec((None, SUBLANES, tn), lambda l, j: (l, 0, j)),
        out_shape=jax.ShapeDtypeStruct((depth, SUBLANES, nm), F32),
        compiler_params=_cparams("parallel", "parallel"),
        name="mod_vectors",
    )(cc, w_mod, b_mod.reshape(depth, 1, nm))


def _dot_3pass(a, b):
    a_hi, a_lo = _split_bf16(a)
    b_hi, b_lo = _split_bf16(b)
    return _dot(a_hi, b_hi) + (_dot(a_hi, b_lo) + _dot(a_lo, b_hi))


def _fold_kernel(cc_ref, sc_ref, wf_ref, o_ref):
    wf = wf_ref[...]
    n = wf.shape[0]
    o_ref[:, :n] = _dot_3pass(cc_ref[...], wf)
    o_ref[:, n:] = _dot_3pass(sc_ref[...], wf)


def _fold_call(cc_bd, sc_bd, wf_bd):
    n = wf_bd.shape[0]
    return pl.pallas_call(
        _fold_kernel,
        out_shape=jax.ShapeDtypeStruct((n, 2 * n), F32),
        name="fourier_weight_fold",
    )(cc_bd, sc_bd, wf_bd)


def _inproj_kernel(*refs, add_pos, d_main):
    if add_pos:
        x_ref, prow_ref, pcol_ref, mod_ref, w_ref, wcs_ref, xo_ref, pm_ref, pq_ref = refs
    else:
        x_ref, mod_ref, w_ref, wcs_ref, pm_ref, pq_ref = refs
    tile, d = x_ref.shape
    m = mod_ref[...]
    sub = min(tile, INPROJ_ROWS)

    def normed(k):
        rows = pl.ds(k * sub, sub)
        if add_pos:
            half = d // 2
            nrow = sub // GRID_W
            prow = prow_ref[pl.ds(k * nrow, nrow), :]
            xr = x_ref[rows, :half].reshape(nrow, GRID_W, half) + prow[:, None, :]
            xq = x_ref[rows, half:].reshape(nrow, GRID_W, half) + pcol_ref[...][None, :, :]
            x = jnp.concatenate([xr.reshape(sub, half), xq.reshape(sub, half)], axis=-1)
            xo_ref[rows, :] = x
        else:
            x = x_ref[rows, :]
        return (_ln(x) * (1.0 + m[1:2, :]) + m[0:1, :]).astype(BF16)

    nsub = tile // sub
    h = normed(0)
    for k in range(nsub):
        rows = pl.ds(k * sub, sub)
        p = _dot(h, w_ref[...])
        if k + 1 < nsub:
            h = normed(k + 1)
        pm_ref[rows, :] = p[:, :d_main].astype(pm_ref.dtype)
        pq_ref[rows, :] = _dot(p[:, d_main:].astype(BF16), wcs_ref[...]).astype(pq_ref.dtype)


def _inproj_call(x, pos, mod, w_in, wcs, tile):
    b, l, d = x.shape
    d_in = w_in.shape[1]
    d_c = wcs.shape[0]
    d_main = d_in - d_c
    add_pos = pos is not None
    tok = lambda bi, i: (bi, i, 0)
    in_specs = [pl.BlockSpec((None, tile, d), tok)]
    args = [x]
    if add_pos:
        prow, pcol = pos
        in_specs.append(pl.BlockSpec((tile // GRID_W, d // 2), lambda bi, i: (i, 0)))
        in_specs.append(pl.BlockSpec((GRID_W, d // 2), lambda bi, i: (0, 0)))
        args += [prow, pcol]
    in_specs += [
        pl.BlockSpec((None, N_MOD, d), lambda bi, i: (bi, 0, 0)),
        pl.BlockSpec((d, d_in), lambda bi, i: (0, 0)),
        pl.BlockSpec((d_c, 2 * d_c), lambda bi, i: (0, 0)),
    ]
    args += [mod, w_in, wcs]
    out_specs = [pl.BlockSpec((None, tile, d_main), tok), pl.BlockSpec((None, tile, 2 * d_c), tok)]
    out_shape = [jax.ShapeDtypeStruct((b, l, d_main), BF16), jax.ShapeDtypeStruct((b, l, 2 * d_c), BF16)]
    if add_pos:
        out_specs.insert(0, pl.BlockSpec((None, tile, d), tok))
        out_shape.insert(0, jax.ShapeDtypeStruct((b, l, d), F32))
    return pl.pallas_call(
        functools.partial(_inproj_kernel, add_pos=add_pos, d_main=d_main),
        grid=(b, l // tile),
        in_specs=in_specs,
        out_specs=out_specs,
        out_shape=out_shape,
        compiler_params=_cparams("parallel", "parallel"),
        name="in_proj",
    )(*args)


def _scan_scratch(tile, d_a):
    G = tile // SUBLANES
    pitch = G + SUBLANES
    return [
        pltpu.VMEM((d_a // LANES, SUBLANES * pitch, LANES), F32),
        pltpu.VMEM((G + 3, SUBLANES, d_a), F32),
        pltpu.VMEM((G, SUBLANES, d_a), F32),
        pltpu.VMEM((G, SUBLANES, d_a), F32),
        pltpu.VMEM((G, SUBLANES, d_a), F32),
        pltpu.VMEM((G, SUBLANES, d_a), F32),
        pltpu.VMEM((SUBLANES, d_a), F32),
    ]


def _scan_tile(x_ref, prev_ref, next_ref, cw_ref, cb_ref, wg_ref, ba_ref, bx_ref, lam_ref,
               out_ref, slab_ref, xt_ref, gr_ref, gi_ref, hl_ref, al_ref, carry_ref,
               *, reverse, seq_first, seq_last):
    tile, d_a = x_ref.shape
    G = tile // SUBLANES
    nq = d_a // LANES
    pitch = slab_ref.shape[1] // SUBLANES
    sub = lax.broadcasted_iota(jnp.int32, (SUBLANES, d_a), 0)
    bc = lambda row: jnp.broadcast_to(row, (SUBLANES, d_a))

    xv = x_ref[...].astype(F32)
    for s in range(SUBLANES):
        for q in range(nq):
            slab_ref[q, s * pitch:s * pitch + G, :] = xv[s * G:(s + 1) * G, q * LANES:(q + 1) * LANES]

    for g in range(G):
        for q in range(nq):
            xt_ref[g + 2, :, q * LANES:(q + 1) * LANES] = slab_ref[q, pl.ds(g, SUBLANES, stride=pitch), :]
    prev = jnp.where(seq_first, 0.0, prev_ref[...].astype(F32))
    nxt = jnp.where(seq_last, 0.0, next_ref[...].astype(F32))
    hrows = prev_ref.shape[0]
    xt_ref[0] = jnp.where(sub == 0, bc(prev[hrows - 2:hrows - 1, :]), pltpu.roll(xt_ref[G], 1, 0))
    xt_ref[1] = jnp.where(sub == 0, bc(prev[hrows - 1:hrows, :]), pltpu.roll(xt_ref[G + 1], 1, 0))
    xt_ref[G + 2] = jnp.where(sub == SUBLANES - 1, bc(nxt[0:1, :]), pltpu.roll(xt_ref[2], SUBLANES - 1, 0))
    cw = cw_ref[...]
    xh = (cb_ref[...][None] + cw[0:1, :][None] * xt_ref[0:G] + cw[1:2, :][None] * xt_ref[1:G + 1]
          + cw[2:3, :][None] * xt_ref[2:G + 2] + cw[3:4, :][None] * xt_ref[3:G + 3])
    xt_ref[0:G] = xh
    xhb = xh.reshape(tile, d_a).astype(BF16)

    pair = 2 * HEAD_DIM
    for j in range(d_a // pair):
        g2 = _dot(xhb[:, j * pair:(j + 1) * pair], wg_ref[j])
        gr_ref[:, :, j * pair:(j + 1) * pair] = g2[:, :pair].reshape(G, SUBLANES, pair)
        gi_ref[:, :, j * pair:(j + 1) * pair] = g2[:, pair:].reshape(G, SUBLANES, pair)
    lam = lam_ref[...]
    neg = -lam
    softplus = jnp.maximum(neg, 0.0) + jnp.log1p(jnp.exp(-jnp.abs(neg)))
    k2 = bc((-0.5 * RG_C / math.log(2.0)) * softplus)
    hba = bc(ba_ref[...])
    hbx = bc(bx_ref[...])

    h = jnp.zeros((SUBLANES, d_a), F32)
    acum = h + 1.0
    for t in range(G):
        g = (G - 1 - t) if reverse else t
        t_r = jnp.tanh(gr_ref[g] + hba)
        t_i = jnp.tanh(gi_ref[g] + hbx)
        l2 = k2 + k2 * t_r
        a = jnp.exp2(l2)
        w = jnp.tanh(l2 * (-math.log(2.0))) * (1.0 + a * a)
        xg = xt_ref[g]
        h = a * h + jnp.sqrt(w) * (xg * t_i + xg)
        acum = acum * a
        hl_ref[g] = h
        al_ref[g] = acum

    first = SUBLANES - 1 if reverse else 0
    shift = SUBLANES - 1 if reverse else 1
    c_in = carry_ref[...]
    c = c_in
    for _ in range(SUBLANES - 1):
        c = jnp.where(sub == first, c_in, pltpu.roll(h + acum * c, shift, 0))
    last = SUBLANES - 1 - first
    carry_ref[...] = bc((h + acum * c)[last:last + 1, :])

    for g in range(G):
        hfull = hl_ref[g] + al_ref[g] * c
        for q in range(nq):
            slab_ref[q, pl.ds(g, SUBLANES, stride=pitch), :] = hfull[:, q * LANES:(q + 1) * LANES]
    for s in range(SUBLANES):
        for q in range(nq):
            out_ref[s * G:(s + 1) * G, q * LANES:(q + 1) * LANES] = (
                slab_ref[q, s * pitch:s * pitch + G, :].astype(out_ref.dtype))


def _lru_kernel(x_ref, prev_ref, next_ref, cw_ref, cb_ref, wg_ref, ba_ref, bx_ref, lam_ref, h0_ref,
                h_ref, hfin_ref, *scratch, reverse):
    i = pl.program_id(1)
    n = pl.num_programs(1)
    li = (n - 1 - i) if reverse else i

    @pl.when(i == 0)
    def _():
        scratch[-1][...] = h0_ref[...]

    _scan_tile(x_ref, prev_ref, next_ref, cw_ref, cb_ref, wg_ref, ba_ref, bx_ref, lam_ref,
               h_ref, *scratch, reverse=reverse, seq_first=li == 0, seq_last=li == n - 1)
    hfin_ref[...] = scratch[-1][...]


def _lru_call(p_main, conv_w_half, conv_b_half, wg, ba_half, bx_half, lam, h0, *, reverse, tile):
    b, l, _ = p_main.shape
    d_a = conv_w_half.shape[-1]
    n = l // tile
    per = tile // HALO
    nblk = l // HALO
    pos = (lambda i: n - 1 - i) if reverse else (lambda i: i)
    vec = lambda shape: pl.BlockSpec(shape, lambda bi, i: (0,) * len(shape))
    return pl.pallas_call(
        functools.partial(_lru_kernel, reverse=reverse),
        grid=(b, n),
        in_specs=[
            pl.BlockSpec((None, tile, d_a), lambda bi, i: (bi, pos(i), 0)),
            pl.BlockSpec((None, HALO, d_a), lambda bi, i: (bi, jnp.maximum(pos(i) * per - 1, 0), 0)),
            pl.BlockSpec((None, HALO, d_a), lambda bi, i: (bi, jnp.minimum((pos(i) + 1) * per, nblk - 1), 0)),
            vec((4, d_a)), vec((1, d_a)), vec(wg.shape), vec((1, d_a)), vec((1, d_a)), vec((1, d_a)),
            pl.BlockSpec((None, SUBLANES, d_a), lambda bi, i: (bi, 0, 0)),
        ],
        out_specs=[
            pl.BlockSpec((None, tile, d_a), lambda bi, i: (bi, pos(i), 0)),
            pl.BlockSpec((None, SUBLANES, d_a), lambda bi, i: (bi, 0, 0)),
        ],
        out_shape=[jax.ShapeDtypeStruct((b, l, d_a), BF16), jax.ShapeDtypeStruct((b, SUBLANES, d_a), F32)],
        scratch_shapes=_scan_scratch(tile, d_a),
        compiler_params=_cparams("arbitrary", "arbitrary"),
        name="lru_scan_bwd" if reverse else "lru_scan_fwd",
    )(p_main, p_main, p_main, conv_w_half, conv_b_half, wg, ba_half, bx_half, lam, h0)


def _dft_direct_kernel(pq_ref, m_ref, y_ref):
    pq = pq_ref[...]
    dc = pq.shape[-1] // 2
    rhs = jnp.concatenate([pq[:, :dc], pq[:, dc:]], axis=0).astype(BF16)
    y_ref[...] = _dot(m_ref[...], rhs).astype(y_ref.dtype)


def _dft_direct_call(pq, m):
    b, l, dc2 = pq.shape
    return pl.pallas_call(
        _dft_direct_kernel,
        grid=(b,),
        in_specs=[pl.BlockSpec((None, l, dc2), lambda bi: (bi, 0, 0)),
                  pl.BlockSpec((l, 2 * l), lambda bi: (0, 0))],
        out_specs=pl.BlockSpec((None, l, dc2 // 2), lambda bi: (bi, 0, 0)),
        out_shape=jax.ShapeDtypeStruct((b, l, dc2 // 2), BF16),
        compiler_params=_cparams("parallel"),
        name="dft_direct",
    )(pq, m)


def _slab_rows(a, width):
    return pl.ds(a * DFT_PITCH, width)


def _dft_stage1_kernel(x_ref, m_ref, ct_ref, st_ref, br_ref, bi_ref, xs_ref, os_ref):
    n1, tl, w = x_ref.shape
    dc = w // 2
    for a in range(n1):
        xa = x_ref[a].astype(F32)
        for q in range(w // LANES):
            xs_ref[q, _slab_rows(a, tl), :] = xa[:, q * LANES:(q + 1) * LANES]
    m = m_ref[...]
    for t in range(tl):
        every = pl.ds(t, n1, stride=DFT_PITCH)
        x = jnp.concatenate([xs_ref[q, every, :] for q in range(w // LANES)], axis=-1).astype(BF16)
        cs = _dot(m, x)
        ar = cs[:n1, :dc] - cs[n1:, dc:]
        ai = cs[:n1, dc:] + cs[n1:, :dc]
        ct = jnp.concatenate([ct_ref[t]] * (dc // LANES), axis=-1)
        st = jnp.concatenate([st_ref[t]] * (dc // LANES), axis=-1)
        br = ct * ar - st * ai
        bi = ct * ai + st * ar
        for q in range(dc // LANES):
            os_ref[0, q, every, :] = br[:, q * LANES:(q + 1) * LANES]
            os_ref[1, q, every, :] = bi[:, q * LANES:(q + 1) * LANES]
    for a in range(n1):
        for part, o_ref in enumerate((br_ref, bi_ref)):
            o_ref[a] = jnp.concatenate([os_ref[part, q, _slab_rows(a, tl), :] for q in range(dc // LANES)],
                                       axis=-1).astype(o_ref.dtype)


def _dft_stage1_call(x4, m1, ct, st):
    b, n1, n2, w = x4.shape
    dc = w // 2
    tl = min(DFT_TILE, n2)
    out_blk = pl.BlockSpec((None, n1, tl, dc), lambda j, bi: (bi, 0, j, 0))
    tab_blk = pl.BlockSpec((tl, n1, LANES), lambda j, bi: (j, 0, 0))
    return pl.pallas_call(
        _dft_stage1_kernel,
        grid=(n2 // tl, b),
        in_specs=[
            pl.BlockSpec((None, n1, tl, w), lambda j, bi: (bi, 0, j, 0)),
            pl.BlockSpec((2 * n1, n1), lambda j, bi: (0, 0)),
            tab_blk, tab_blk,
        ],
        out_specs=[out_blk, out_blk],
        out_shape=[jax.ShapeDtypeStruct((b, n1, n2, dc), BF16)] * 2,
        scratch_shapes=[pltpu.VMEM((w // LANES, n1 * DFT_PITCH, LANES), F32),
                        pltpu.VMEM((2, dc // LANES, n1 * DFT_PITCH, LANES), F32)],
        compiler_params=_cparams("parallel", "parallel"),
        name="dft_stage1",
    )(x4, m1, ct, st)


def _dft_stage2_kernel(br_ref, bi_ref, m_ref, y_ref, ys_ref):
    tk, n2, dc = br_ref.shape
    m = m_ref[...]
    for a in range(tk):
        y = _dot(m, jnp.concatenate([br_ref[a], bi_ref[a]], axis=0).astype(BF16))
        for q in range(dc // LANES):
            ys_ref[q, pl.ds(a, n2, stride=DFT_PITCH), :] = y[:, q * LANES:(q + 1) * LANES]
    for k2 in range(n2):
        y_ref[k2] = jnp.concatenate([ys_ref[q, _slab_rows(k2, tk), :] for q in range(dc // LANES)],
                                    axis=-1).astype(y_ref.dtype)


def _dft_stage2_call(br, bi, m2):
    b, n1, n2, dc = br.shape
    tk = min(DFT_TILE, n1)
    blk = pl.BlockSpec((None, tk, n2, dc), lambda bi_, k: (bi_, k, 0, 0))
    return pl.pallas_call(
        _dft_stage2_kernel,
        grid=(b, n1 // tk),
        in_specs=[blk, blk, pl.BlockSpec((n2, 2 * n2), lambda bi_, k: (0, 0))],
        out_specs=pl.BlockSpec((None, n2, tk, dc), lambda bi_, k: (bi_, 0, k, 0)),
        out_shape=jax.ShapeDtypeStruct((b, n2, n1, dc), BF16),
        scratch_shapes=[pltpu.VMEM((dc // LANES, n2 * DFT_PITCH, LANES), F32)],
        compiler_params=_cparams("parallel", "parallel"),
        name="dft_stage2",
    )(br, bi, m2)


def _dft_tables(l, scale):
    n2 = DFT_N2
    n1 = l // n2

    def cs(num, den):
        ang = (2.0 * np.pi / den) * (num % den).astype(np.float64)
        return np.cos(ang), np.sin(ang)

    k1 = np.arange(n1)
    c1, s1 = cs(np.outer(k1, k1), n1)
    l2 = np.arange(n2)
    ct, st = cs(np.outer(k1, l2), l)
    c2, s2 = cs(np.outer(l2, l2), n2)
    return c1, s1, ct, st, c2 * scale, s2 * scale


def _fourier_latent(pq, dc):
    b, l, _ = pq.shape
    n2 = DFT_N2
    n1 = l // n2
    c1, s1, ct, st, c2, s2 = _dft_tables(l, 1.0 / math.sqrt(l * HEAD_DIM))
    m1 = jnp.asarray(np.concatenate([c1, s1], axis=0), F32).astype(BF16)
    m2 = jnp.asarray(np.concatenate([c2, -s2], axis=1), F32).astype(BF16)
    lanes = lambda t: jnp.asarray(np.broadcast_to(t.T[:, :, None], (n2, n1, LANES)), F32)
    br, bi = _dft_stage1_call(pq.reshape(b, n1, n2, 2 * dc), m1, lanes(ct), lanes(st))
    y = _dft_stage2_call(br, bi, m2)
    return y.reshape(b, l, dc)


def _fourier_direct(pq, dc):
    b, l, _ = pq.shape
    k = np.arange(l)
    ang = (2.0 * np.pi / l) * (np.outer(k, k) % l).astype(np.float64)
    scale = 1.0 / math.sqrt(l * HEAD_DIM)
    m = jnp.asarray(np.concatenate([np.cos(ang) * scale, -np.sin(ang) * scale], axis=1), F32).astype(BF16)
    return _dft_direct_call(pq, m)


MIX_STAGES = 5


def _mix_rows(rows, out_ref, gate_ref, uv_ref, hf_ref, hb_ref, yc_ref, x_ref, mod_ref, seg_ref, ws_ref, sgb_ref,
              gmix_ref, wout_ref, g_ref, b_ref, *, alpha, d_a, d_b):
    seg = seg_ref[...]
    inv = 1.0 / HEAD_DIM

    def seg_mean(hi, lo):
        return (_dot(hi, seg) + _dot(lo, seg)) * inv

    def rms(t, g):
        return t * lax.rsqrt(jnp.mean(t * t, axis=-1, keepdims=True) + LN_EPS) * g

    gm = gmix_ref[...]
    z = _gelu(uv_ref[rows, :].astype(F32))
    u = z[:, :d_b]
    v = z[:, d_b:]
    v_hi, v_lo = _split_bf16(v)
    yield
    dv = v - seg_mean(v_hi, v_lo)
    q_hi, q_lo = _split_bf16(dv * dv)
    ya = rms(_gelu(gate_ref[rows, :].astype(F32)) * (hf_ref[rows, :].astype(F32) + hb_ref[rows, :].astype(F32)),
             gm[:, :d_a])
    yield
    vn = dv * lax.rsqrt(seg_mean(q_hi, q_lo) + LN_EPS)
    lane = lax.broadcasted_iota(jnp.int32, (CHUNK, d_b), 1)
    heads = d_b // HEAD_DIM
    rhs = jnp.concatenate(
        [jnp.where((lane >= hh * HEAD_DIM) & (lane < (hh + 1) * HEAD_DIM), vn, 0.0) for hh in range(heads)],
        axis=0).astype(BF16)
    yc = rms(yc_ref[rows, :].astype(F32), gm[:, d_a + d_b:])
    yield
    yb = rms(u * (_dot(ws_ref[...], rhs) + sgb_ref[...]), gm[:, d_a:d_a + d_b])
    y = jnp.concatenate([ya, yb, yc], axis=-1).astype(BF16)
    yield
    o = _dot(y, wout_ref[...])
    m = mod_ref[...]
    out_ref[rows, :] = _ln(alpha * x_ref[rows, :] + m[2:3, :] * o) * g_ref[...] + b_ref[...]
    yield


def _ffn_body(x, m, wup_ref, wdn_ref, g_ref, b_ref, res_ref, between, *, alpha):
    res_ref[...] = x
    h = (_ln(x) * (1.0 + m[4:5, :]) + m[3:4, :]).astype(BF16)
    for j in range(wup_ref.shape[0]):
        gu = _dot(h, wup_ref[j])
        g = gu[:, :FF_CHUNK]
        act = (g * jax.nn.sigmoid(g) * gu[:, FF_CHUNK:]).astype(BF16)
        part = _dot(act, wdn_ref[j])
        acc = part if j == 0 else acc + part
        between(j)
    return _ln(alpha * res_ref[...] + m[5:6, :] * acc) * g_ref[...] + b_ref[...]


def _mixffn_kernel(gate_ref, uv_ref, hf_ref, hb_ref, yc_ref, x_ref, modm_ref, modf_ref, seg_ref, ws_ref, sgb_ref,
                   gmix_ref, wout_ref, g1_ref, b1_ref, wup_ref, wdn_ref, g2_ref, b2_ref,
                   o_ref, x1_ref, res_ref, *, alpha, d_a, d_b):
    @pl.when(pl.program_id(0) == 0)
    def _():
        x1_ref[...] = jnp.zeros_like(x1_ref)

    nchunk = x_ref.shape[0] // CHUNK
    nff = wup_ref.shape[0]
    todo = []
    for k0 in range(0, nchunk, 2):
        gens = [_mix_rows(pl.ds(c * CHUNK, CHUNK), x1_ref, gate_ref, uv_ref, hf_ref, hb_ref, yc_ref, x_ref,
                          modm_ref, seg_ref, ws_ref, sgb_ref, gmix_ref, wout_ref, g1_ref, b1_ref,
                          alpha=alpha, d_a=d_a, d_b=d_b) for c in range(k0, min(k0 + 2, nchunk))]
        todo += [g for _ in range(MIX_STAGES) for g in gens]
    per_gap = -(-len(todo) // (nff - 1))

    def between(j):
        count = len(todo) if j == nff - 1 else per_gap
        for _ in range(min(count, len(todo))):
            next(todo.pop(0))

    o_ref[...] = _ffn_body(x1_ref[...], modf_ref[...], wup_ref, wdn_ref, g2_ref, b2_ref, res_ref, between,
                           alpha=alpha)


def _mixffn_call(p_main, hf, hb, yc, x, mod, seg, ws_cat, sgb, g_mix, w_out, g1, b1, w_up_r, w_dn_r, g2, b2,
                 *, alpha, tile):
    b, l, d = x.shape
    d_a = hf.shape[-1]
    d_c = yc.shape[-1]
    d_b = d_c
    n = l // tile
    total = b * n

    def mixed(j):
        def index(s):
            t = jnp.minimum(s, total - 1)
            return (t // n, t % n, j)
        return index

    def fed(s):
        t = jnp.maximum(s - 1, 0)
        return (t // n, t % n, 0)

    vec = lambda shape: pl.BlockSpec(shape, lambda s: (0,) * len(shape))
    res = lambda shape: pl.BlockSpec(shape, lambda s: (0,) * len(shape), pipeline_mode=pl.Buffered(1))
    return pl.pallas_call(
        functools.partial(_mixffn_kernel, alpha=alpha, d_a=d_a, d_b=d_b),
        grid=(total + 1,),
        in_specs=[
            pl.BlockSpec((None, tile, d_a), mixed(1)),
            pl.BlockSpec((None, tile, 2 * d_b), mixed(2)),
            pl.BlockSpec((None, tile, d_a), mixed(0)),
            pl.BlockSpec((None, tile, d_a), mixed(0)),
            pl.BlockSpec((None, tile, d_c), mixed(0)),
            pl.BlockSpec((None, tile, d), mixed(0)),
            pl.BlockSpec((None, N_MOD, d), lambda s: (jnp.minimum(s, total - 1) // n, 0, 0)),
            pl.BlockSpec((None, N_MOD, d), lambda s: (jnp.maximum(s - 1, 0) // n, 0, 0)),
            vec(seg.shape), vec(ws_cat.shape), vec(sgb.shape), vec((1, d)), vec(w_out.shape),
            vec((1, d)), vec((1, d)),
            res(w_up_r.shape), res(w_dn_r.shape), vec((1, d)), vec((1, d)),
        ],
        out_specs=pl.BlockSpec((None, tile, d), fed),
        out_shape=jax.ShapeDtypeStruct((b, l, d), F32),
        scratch_shapes=[pltpu.VMEM((tile, d), F32)] * 2,
        compiler_params=_cparams("arbitrary"),
        name="mix_ffn",
    )(p_main, p_main, hf, hb, yc, x, mod, mod, seg, ws_cat, sgb, g_mix, w_out, g1, b1, w_up_r, w_dn_r, g2, b2)


def _pos_tables(rows, dim):
    quarter = dim // 4
    freqs = POS_BASE ** (-jnp.arange(quarter, dtype=F32) / quarter)

    def enc(p):
        ang = p[:, None] * freqs[None, :]
        return jnp.concatenate([jnp.sin(ang), jnp.cos(ang)], -1)

    return enc(jnp.arange(rows, dtype=F32)), enc(jnp.arange(GRID_W, dtype=F32))


def _block_diag(blocks):
    n, r, c = blocks.shape
    eye = jnp.eye(n, dtype=blocks.dtype)
    return (eye[:, None, :, None] * blocks[:, :, None, :]).reshape(n * r, n * c)


def _gate_weights(wa, wx):
    heads = wa.shape[0]
    pa = jnp.stack([_block_diag(wa[2 * j:2 * j + 2]) for j in range(heads // 2)])
    px = jnp.stack([_block_diag(wx[2 * j:2 * j + 2]) for j in range(heads // 2)])
    return jnp.concatenate([pa, px], axis=-1).astype(BF16)


def kernel(x, c, ctx, c_ctx, w_mod, b_mod, w_in, conv_w, conv_b, lru_wa, lru_ba, lru_wx, lru_bx, lru_lam,
           sg_ws, sg_b, fourier_w, g_mix, w_out, ln1_g, ln1_b, w_up, w_down, ln2_g, ln2_b):
    bsz, seq, d = x.shape
    ctx_len = ctx.shape[1]
    depth = w_mod.shape[0]
    d_a = conv_w.shape[-1]
    d_c = fourier_w.shape[1] * fourier_w.shape[2]
    d_ff = w_down.shape[1]
    alpha = (2 * depth) ** 0.25
    lat_tile = min(512, seq)
    inproj_tile = min(2 * INPROJ_ROWS, seq)
    ctx_tile = min(256, ctx_len)

    cc = jnp.zeros((SUBLANES, d), F32).at[:bsz].set(c).at[bsz].set(c_ctx)
    mod = _mod_call(cc, w_mod, b_mod).reshape(depth, SUBLANES, N_MOD, d)

    pos = _pos_tables(seq // GRID_W, d)
    k = np.arange(HEAD_DIM)
    ang = (2.0 * np.pi / HEAD_DIM) * (np.outer(k, k) % HEAD_DIM).astype(np.float64)
    groups_c = d_c // HEAD_DIM
    cc_bd = jnp.asarray(np.kron(np.eye(groups_c), np.cos(ang)), F32)
    sc_bd = jnp.asarray(np.kron(np.eye(groups_c), np.sin(ang)), F32)
    seg = jnp.asarray(np.kron(np.eye(groups_c), np.ones((HEAD_DIM, HEAD_DIM))), BF16)
    zeros_state = jnp.zeros((bsz, SUBLANES, d_a), F32)

    xl, xc = x, ctx
    for l in range(depth):
        last = l == depth - 1
        mod_l = mod[l, :bsz]
        mod_c = jnp.broadcast_to(mod[l, bsz][None], (bsz, N_MOD, d))
        w_in_b = w_in[l].astype(BF16)
        wcs = _fold_call(cc_bd, sc_bd, _block_diag(fourier_w[l])).astype(BF16)
        cw, cb = 0.5 * conv_w[l], 0.5 * conv_b[l].reshape(1, d_a)
        lru = [(cw, cb, _gate_weights(lru_wa[l, dd], lru_wx[l, dd]), 0.5 * lru_ba[l, dd].reshape(1, d_a),
                0.5 * lru_bx[l, dd].reshape(1, d_a), lru_lam[l, dd].reshape(1, d_a)) for dd in range(2)]
        ws_cat = jnp.concatenate([sg_ws[l, hh] for hh in range(sg_ws.shape[1])], axis=1).astype(BF16)
        sgb = jnp.repeat(jnp.transpose(sg_b[l]), HEAD_DIM, axis=1)
        w_out_b = w_out[l].astype(BF16)
        nff = d_ff // FF_CHUNK
        w_up_r = jnp.concatenate([w_up[l][:, :d_ff].reshape(d, nff, FF_CHUNK),
                                  w_up[l][:, d_ff:].reshape(d, nff, FF_CHUNK)], axis=-1)
        w_up_r = jnp.transpose(w_up_r, (1, 0, 2)).astype(BF16)
        w_dn_r = w_down[l].reshape(nff, FF_CHUNK, d).astype(BF16)
        g1, b1 = ln1_g[l].reshape(1, d), ln1_b[l].reshape(1, d)
        g2, b2 = ln2_g[l].reshape(1, d), ln2_b[l].reshape(1, d)
        gm = g_mix[l].reshape(1, d)

        def mix_ffn(p_main, hf, hb, yc, xs, mods, tile):
            return _mixffn_call(p_main, hf, hb, yc, xs, mods, seg, ws_cat, sgb, gm, w_out_b, g1, b1,
                                w_up_r, w_dn_r, g2, b2, alpha=alpha, tile=tile)

        pc_main, pc_pq = _inproj_call(xc, None, mod_c, w_in_b, wcs, ctx_tile)
        hf_c, sf_c = _lru_call(pc_main, *lru[0], zeros_state, reverse=False, tile=ctx_tile)
        hb_c, sb_c = _lru_call(pc_main, *lru[1], zeros_state, reverse=True, tile=ctx_tile)
        if not last:
            xc = mix_ffn(pc_main, hf_c, hb_c, _fourier_direct(pc_pq, d_c), xc, mod_c, ctx_tile)

        if l == 0:
            xl, pl_main, pl_pq = _inproj_call(xl, pos, mod_l, w_in_b, wcs, inproj_tile)
        else:
            pl_main, pl_pq = _inproj_call(xl, None, mod_l, w_in_b, wcs, inproj_tile)
        hf, _ = _lru_call(pl_main, *lru[0], sf_c, reverse=False, tile=lat_tile)
        hb, _ = _lru_call(pl_main, *lru[1], sb_c, reverse=True, tile=lat_tile)
        if seq > 2 * DFT_N2:
            yc = _fourier_latent(pl_pq, d_c)
        else:
            yc = _fourier_direct(pl_pq, d_c)
        xl = mix_ffn(pl_main, hf, hb, yc, xl, mod_l, lat_tile)
    return xl
```

```python
import functools
import math

import jax
import jax.numpy as jnp
import numpy as np
from jax import lax
from jax.experimental import pallas as pl
from jax.experimental.pallas import tpu as pltpu

HEAD_DIM = 64
CHUNK = 128
GRID_W = 64
RG_C = 8.0
LN_EPS = 1e-6
POS_BASE = 10000.0
N_MOD = 6
SUBLANES = 8
LANES = 128
HALO = 2 * SUBLANES
FF_CHUNK = 256
DFT_N2 = 128
DFT_TILE = HALO
DFT_PITCH = DFT_TILE + SUBLANES
INPROJ_ROWS = 512
VMEM_LIMIT = 56 * 1024 * 1024

F32 = jnp.float32
BF16 = jnp.bfloat16


def _dot(a, b):
    return jnp.dot(a, b, preferred_element_type=F32)


def _ln(x):
    mu = jnp.mean(x, axis=-1, keepdims=True)
    d = x - mu
    var = jnp.mean(d * d, axis=-1, keepdims=True)
    return d * lax.rsqrt(var + LN_EPS)


def _gelu(x):
    return 0.5 * x * (1.0 + lax.erf(x * (1.0 / math.sqrt(2.0))))


def _cparams(*sem):
    return pltpu.CompilerParams(dimension_semantics=sem, vmem_limit_bytes=VMEM_LIMIT)


def _split_bf16(t):
    hi = t.astype(BF16)
    return hi, (t - hi.astype(F32)).astype(BF16)


def _mod_kernel(c_ref, w_ref, b_ref, o_ref):
    c = c_ref[...]
    s = (c * jax.nn.sigmoid(c)).astype(BF16)
    o_ref[...] = _dot(s, w_ref[...].astype(BF16)) + b_ref[...]


def _mod_call(cc, w_mod, b_mod):
    depth, d, nm = w_mod.shape
    tn = 1536
    return pl.pallas_call(
        _mod_kernel,
        grid=(depth, nm // tn),
        in_specs=[
            pl.BlockSpec((SUBLANES, d), lambda l, j: (0, 0)),
            pl.BlockSpec((None, d, tn), lambda l, j: (l, 0, j)),
            pl.BlockSpec((None, 1, tn), lambda l, j: (l, 0, j)),
        ],
        out_specs=pl.BlockSpec((None, SUBLANES, tn), lambda l, j: (l, 0, j)),
        out_shape=jax.ShapeDtypeStruct((depth, SUBLANES, nm), F32),
        compiler_params=_cparams("parallel", "parallel"),
        name="mod_vectors",
    )(cc, w_mod, b_mod.reshape(depth, 1, nm))


def _dot_3pass(a, b):
    a_hi, a_lo = _split_bf16(a)
    b_hi, b_lo = _split_bf16(b)
    return _dot(a_hi, b_hi) + (_dot(a_hi, b_lo) + _dot(a_lo, b_hi))


def _fold_kernel(cc_ref, sc_ref, wf_ref, o_ref):
    wf = wf_ref[...]
    n = wf.shape[0]
    o_ref[:, :n] = _dot_3pass(cc_ref[...], wf)
    o_ref[:, n:] = _dot_3pass(sc_ref[...], wf)


def _fold_call(cc_bd, sc_bd, wf_bd):
    n = wf_bd.shape[0]
    return pl.pallas_call(
        _fold_kernel,
        out_shape=jax.ShapeDtypeStruct((n, 2 * n), F32),
        name="fourier_weight_fold",
    )(cc_bd, sc_bd, wf_bd)


def _inproj_kernel(*refs, add_pos, d_main):
    if add_pos:
        x_ref, prow_ref, pcol_ref, mod_ref, w_ref, wcs_ref, xo_ref, pm_ref, pq_ref = refs
    else:
        x_ref, mod_ref, w_ref, wcs_ref, pm_ref, pq_ref = refs
    tile, d = x_ref.shape
    m = mod_ref[...]
    sub = min(tile, INPROJ_ROWS)

    def normed(k):
        rows = pl.ds(k * sub, sub)
        if add_pos:
            half = d // 2
            nrow = sub // GRID_W
            prow = prow_ref[pl.ds(k * nrow, nrow), :]
            xr = x_ref[rows, :half].reshape(nrow, GRID_W, half) + prow[:, None, :]
            xq = x_ref[rows, half:].reshape(nrow, GRID_W, half) + pcol_ref[...][None, :, :]
            x = jnp.concatenate([xr.reshape(sub, half), xq.reshape(sub, half)], axis=-1)
            xo_ref[rows, :] = x
        else:
            x = x_ref[rows, :]
        return (_ln(x) * (1.0 + m[1:2, :]) + m[0:1, :]).astype(BF16)

    nsub = tile // sub
    h = normed(0)
    for k in range(nsub):
        rows = pl.ds(k * sub, sub)
        p = _dot(h, w_ref[...])
        if k + 1 < nsub:
            h = normed(k + 1)
        pm_ref[rows, :] = p[:, :d_main].astype(pm_ref.dtype)
        pq_ref[rows, :] = _dot(p[:, d_main:].astype(BF16), wcs_ref[...]).astype(pq_ref.dtype)


def _inproj_call(x, pos, mod, w_in, layer, wcs, tile):
    b, l, d = x.shape
    d_in = w_in.shape[2]
    d_c = wcs.shape[0]
    d_main = d_in - d_c
    add_pos = pos is not None
    tok = lambda bi, i: (bi, i, 0)
    in_specs = [pl.BlockSpec((None, tile, d), tok)]
    args = [x]
    if add_pos:
        prow, pcol = pos
        in_specs.append(pl.BlockSpec((tile // GRID_W, d // 2), lambda bi, i: (i, 0)))
        in_specs.append(pl.BlockSpec((GRID_W, d // 2), lambda bi, i: (0, 0)))
        args += [prow, pcol]
    in_specs += [
        pl.BlockSpec((None, N_MOD, d), lambda bi, i: (bi, 0, 0)),
        pl.BlockSpec((None, d, d_in), lambda bi, i: (layer, 0, 0)),
        pl.BlockSpec((d_c, 2 * d_c), lambda bi, i: (0, 0)),
    ]
    args += [mod, w_in, wcs]
    out_specs = [pl.BlockSpec((None, tile, d_main), tok), pl.BlockSpec((None, tile, 2 * d_c), tok)]
    out_shape = [jax.ShapeDtypeStruct((b, l, d_main), BF16), jax.ShapeDtypeStruct((b, l, 2 * d_c), BF16)]
    if add_pos:
        out_specs.insert(0, pl.BlockSpec((None, tile, d), tok))
        out_shape.insert(0, jax.ShapeDtypeStruct((b, l, d), F32))
    return pl.pallas_call(
        functools.partial(_inproj_kernel, add_pos=add_pos, d_main=d_main),
        grid=(b, l // tile),
        in_specs=in_specs,
        out_specs=out_specs,
        out_shape=out_shape,
        compiler_params=_cparams("parallel", "parallel"),
        name="in_proj",
    )(*args)


def _scan_scratch(tile, d_a):
    G = tile // SUBLANES
    pitch = G + SUBLANES
    return [
        pltpu.VMEM((d_a // LANES, SUBLANES * pitch, LANES), F32),
        pltpu.VMEM((G + 3, SUBLANES, d_a), F32),
        pltpu.VMEM((G, SUBLANES, d_a), F32),
        pltpu.VMEM((G, SUBLANES, d_a), F32),
        pltpu.VMEM((G, SUBLANES, d_a), F32),
        pltpu.VMEM((G, SUBLANES, d_a), F32),
        pltpu.VMEM((SUBLANES, d_a), F32),
    ]


def _scan_tile(x_ref, prev_ref, next_ref, cw_ref, cb_ref, wg_ref, ba_ref, bx_ref, lam_ref,
               out_ref, slab_ref, xt_ref, gr_ref, gi_ref, hl_ref, al_ref, carry_ref,
               *, reverse, seq_first, seq_last):
    tile, d_a = x_ref.shape
    G = tile // SUBLANES
    nq = d_a // LANES
    pitch = slab_ref.shape[1] // SUBLANES
    sub = lax.broadcasted_iota(jnp.int32, (SUBLANES, d_a), 0)
    bc = lambda row: jnp.broadcast_to(row, (SUBLANES, d_a))

    xv = x_ref[...].astype(F32)
    for s in range(SUBLANES):
        for q in range(nq):
            slab_ref[q, s * pitch:s * pitch + G, :] = xv[s * G:(s + 1) * G, q * LANES:(q + 1) * LANES]

    for g in range(G):
        for q in range(nq):
            xt_ref[g + 2, :, q * LANES:(q + 1) * LANES] = slab_ref[q, pl.ds(g, SUBLANES, stride=pitch), :]
    prev = jnp.where(seq_first, 0.0, prev_ref[...].astype(F32))
    nxt = jnp.where(seq_last, 0.0, next_ref[...].astype(F32))
    hrows = prev_ref.shape[0]
    xt_ref[0] = jnp.where(sub == 0, bc(prev[hrows - 2:hrows - 1, :]), pltpu.roll(xt_ref[G], 1, 0))
    xt_ref[1] = jnp.where(sub == 0, bc(prev[hrows - 1:hrows, :]), pltpu.roll(xt_ref[G + 1], 1, 0))
    xt_ref[G + 2] = jnp.where(sub == SUBLANES - 1, bc(nxt[0:1, :]), pltpu.roll(xt_ref[2], SUBLANES - 1, 0))
    cw = cw_ref[...]
    xh = (cb_ref[...][None] + cw[0:1, :][None] * xt_ref[0:G] + cw[1:2, :][None] * xt_ref[1:G + 1]
          + cw[2:3, :][None] * xt_ref[2:G + 2] + cw[3:4, :][None] * xt_ref[3:G + 3])
    xt_ref[0:G] = xh
    xhb = xh.reshape(tile, d_a).astype(BF16)

    pair = 2 * HEAD_DIM
    for j in range(d_a // pair):
        g2 = _dot(xhb[:, j * pair:(j + 1) * pair], wg_ref[j])
        gr_ref[:, :, j * pair:(j + 1) * pair] = g2[:, :pair].reshape(G, SUBLANES, pair)
        gi_ref[:, :, j * pair:(j + 1) * pair] = g2[:, pair:].reshape(G, SUBLANES, pair)
    lam = lam_ref[...]
    neg = -lam
    softplus = jnp.maximum(neg, 0.0) + jnp.log1p(jnp.exp(-jnp.abs(neg)))
    k2 = bc((-0.5 * RG_C / math.log(2.0)) * softplus)
    hba = bc(ba_ref[...])
    hbx = bc(bx_ref[...])

    h = jnp.zeros((SUBLANES, d_a), F32)
    acum = h + 1.0
    for t in range(G):
        g = (G - 1 - t) if reverse else t
        t_r = jnp.tanh(gr_ref[g] + hba)
        t_i = jnp.tanh(gi_ref[g] + hbx)
        l2 = k2 + k2 * t_r
        a = jnp.exp2(l2)
        w = jnp.tanh(l2 * (-math.log(2.0))) * (1.0 + a * a)
        xg = xt_ref[g]
        root = jnp.where(w > 0.0, w * lax.rsqrt(w), 0.0)
        h = a * h + root * (xg * t_i + xg)
        acum = acum * a
        hl_ref[g] = h
        al_ref[g] = acum

    first = SUBLANES - 1 if reverse else 0
    shift = SUBLANES - 1 if reverse else 1
    c_in = carry_ref[...]
    c = c_in
    for _ in range(SUBLANES - 1):
        c = jnp.where(sub == first, c_in, pltpu.roll(h + acum * c, shift, 0))
    last = SUBLANES - 1 - first
    carry_ref[...] = bc((h + acum * c)[last:last + 1, :])

    for g in range(G):
        hfull = hl_ref[g] + al_ref[g] * c
        for q in range(nq):
            slab_ref[q, pl.ds(g, SUBLANES, stride=pitch), :] = hfull[:, q * LANES:(q + 1) * LANES]
    for s in range(SUBLANES):
        for q in range(nq):
            out_ref[s * G:(s + 1) * G, q * LANES:(q + 1) * LANES] = (
                slab_ref[q, s * pitch:s * pitch + G, :].astype(out_ref.dtype))


def _lru_kernel(x_ref, prev_ref, next_ref, cw_ref, cb_ref, wg_ref, ba_ref, bx_ref, lam_ref, h0_ref,
                h_ref, hfin_ref, *scratch, reverse):
    i = pl.program_id(1)
    n = pl.num_programs(1)
    li = (n - 1 - i) if reverse else i

    @pl.when(i == 0)
    def _():
        scratch[-1][...] = h0_ref[...]

    _scan_tile(x_ref, prev_ref, next_ref, cw_ref, cb_ref, wg_ref, ba_ref, bx_ref, lam_ref,
               h_ref, *scratch, reverse=reverse, seq_first=li == 0, seq_last=li == n - 1)
    hfin_ref[...] = scratch[-1][...]


def _lru_call(p_main, conv_w_half, conv_b_half, wg, ba_half, bx_half, lam, h0, *, reverse, tile):
    b, l, _ = p_main.shape
    d_a = conv_w_half.shape[-1]
    n = l // tile
    per = tile // HALO
    nblk = l // HALO
    pos = (lambda i: n - 1 - i) if reverse else (lambda i: i)
    vec = lambda shape: pl.BlockSpec(shape, lambda bi, i: (0,) * len(shape))
    return pl.pallas_call(
        functools.partial(_lru_kernel, reverse=reverse),
        grid=(b, n),
        in_specs=[
            pl.BlockSpec((None, tile, d_a), lambda bi, i: (bi, pos(i), 0)),
            pl.BlockSpec((None, HALO, d_a), lambda bi, i: (bi, jnp.maximum(pos(i) * per - 1, 0), 0)),
            pl.BlockSpec((None, HALO, d_a), lambda bi, i: (bi, jnp.minimum((pos(i) + 1) * per, nblk - 1), 0)),
            vec((4, d_a)), vec((1, d_a)), vec(wg.shape), vec((1, d_a)), vec((1, d_a)), vec((1, d_a)),
            pl.BlockSpec((None, SUBLANES, d_a), lambda bi, i: (bi, 0, 0)),
        ],
        out_specs=[
            pl.BlockSpec((None, tile, d_a), lambda bi, i: (bi, pos(i), 0)),
            pl.BlockSpec((None, SUBLANES, d_a), lambda bi, i: (bi, 0, 0)),
        ],
        out_shape=[jax.ShapeDtypeStruct((b, l, d_a), BF16), jax.ShapeDtypeStruct((b, SUBLANES, d_a), F32)],
        scratch_shapes=_scan_scratch(tile, d_a),
        compiler_params=_cparams("arbitrary", "arbitrary"),
        name="lru_scan_bwd" if reverse else "lru_scan_fwd",
    )(p_main, p_main, p_main, conv_w_half, conv_b_half, wg, ba_half, bx_half, lam, h0)


def _dft_direct_kernel(pq_ref, m_ref, y_ref):
    pq = pq_ref[...]
    dc = pq.shape[-1] // 2
    rhs = jnp.concatenate([pq[:, :dc], pq[:, dc:]], axis=0).astype(BF16)
    y_ref[...] = _dot(m_ref[...], rhs).astype(y_ref.dtype)


def _dft_direct_call(pq, m):
    b, l, dc2 = pq.shape
    return pl.pallas_call(
        _dft_direct_kernel,
        grid=(b,),
        in_specs=[pl.BlockSpec((None, l, dc2), lambda bi: (bi, 0, 0)),
                  pl.BlockSpec((l, 2 * l), lambda bi: (0, 0))],
        out_specs=pl.BlockSpec((None, l, dc2 // 2), lambda bi: (bi, 0, 0)),
        out_shape=jax.ShapeDtypeStruct((b, l, dc2 // 2), BF16),
        compiler_params=_cparams("parallel"),
        name="dft_direct",
    )(pq, m)


def _slab_rows(a, width):
    return pl.ds(a * DFT_PITCH, width)


def _dft_stage1_kernel(x_ref, m_ref, ct_ref, st_ref, br_ref, bi_ref, xs_ref, os_ref):
    n1, tl, w = x_ref.shape
    dc = w // 2
    for a in range(n1):
        xa = x_ref[a].astype(F32)
        for q in range(w // LANES):
            xs_ref[q, _slab_rows(a, tl), :] = xa[:, q * LANES:(q + 1) * LANES]
    m = m_ref[...]
    for t in range(tl):
        every = pl.ds(t, n1, stride=DFT_PITCH)
        x = jnp.concatenate([xs_ref[q, every, :] for q in range(w // LANES)], axis=-1).astype(BF16)
        cs = _dot(m, x)
        ar = cs[:n1, :dc] - cs[n1:, dc:]
        ai = cs[:n1, dc:] + cs[n1:, :dc]
        ct = jnp.concatenate([ct_ref[t]] * (dc // LANES), axis=-1)
        st = jnp.concatenate([st_ref[t]] * (dc // LANES), axis=-1)
        br = ct * ar - st * ai
        bi = ct * ai + st * ar
        for q in range(dc // LANES):
            os_ref[0, q, every, :] = br[:, q * LANES:(q + 1) * LANES]
            os_ref[1, q, every, :] = bi[:, q * LANES:(q + 1) * LANES]
    for a in range(n1):
        for part, o_ref in enumerate((br_ref, bi_ref)):
            o_ref[a] = jnp.concatenate([os_ref[part, q, _slab_rows(a, tl), :] for q in range(dc // LANES)],
                                       axis=-1).astype(o_ref.dtype)


def _dft_stage1_call(x4, m1, ct, st):
    b, n1, n2, w = x4.shape
    dc = w // 2
    tl = min(DFT_TILE, n2)
    out_blk = pl.BlockSpec((None, n1, tl, dc), lambda j, bi: (bi, 0, j, 0))
    tab_blk = pl.BlockSpec((tl, n1, LANES), lambda j, bi: (j, 0, 0))
    return pl.pallas_call(
        _dft_stage1_kernel,
        grid=(n2 // tl, b),
        in_specs=[
            pl.BlockSpec((None, n1, tl, w), lambda j, bi: (bi, 0, j, 0)),
            pl.BlockSpec((2 * n1, n1), lambda j, bi: (0, 0)),
            tab_blk, tab_blk,
        ],
        out_specs=[out_blk, out_blk],
        out_shape=[jax.ShapeDtypeStruct((b, n1, n2, dc), BF16)] * 2,
        scratch_shapes=[pltpu.VMEM((w // LANES, n1 * DFT_PITCH, LANES), F32),
                        pltpu.VMEM((2, dc // LANES, n1 * DFT_PITCH, LANES), F32)],
        compiler_params=_cparams("parallel", "parallel"),
        name="dft_stage1",
    )(x4, m1, ct, st)


def _dft_stage2_kernel(br_ref, bi_ref, m_ref, y_ref, ys_ref):
    tk, n2, dc = br_ref.shape
    m = m_ref[...]
    for a in range(tk):
        y = _dot(m, jnp.concatenate([br_ref[a], bi_ref[a]], axis=0).astype(BF16))
        for q in range(dc // LANES):
            ys_ref[q, pl.ds(a, n2, stride=DFT_PITCH), :] = y[:, q * LANES:(q + 1) * LANES]
    for k2 in range(n2):
        y_ref[k2] = jnp.concatenate([ys_ref[q, _slab_rows(k2, tk), :] for q in range(dc // LANES)],
                                    axis=-1).astype(y_ref.dtype)


def _dft_stage2_call(br, bi, m2):
    b, n1, n2, dc = br.shape
    tk = min(DFT_TILE, n1)
    blk = pl.BlockSpec((None, tk, n2, dc), lambda bi_, k: (bi_, k, 0, 0))
    return pl.pallas_call(
        _dft_stage2_kernel,
        grid=(b, n1 // tk),
        in_specs=[blk, blk, pl.BlockSpec((n2, 2 * n2), lambda bi_, k: (0, 0))],
        out_specs=pl.BlockSpec((None, n2, tk, dc), lambda bi_, k: (bi_, 0, k, 0)),
        out_shape=jax.ShapeDtypeStruct((b, n2, n1, dc), BF16),
        scratch_shapes=[pltpu.VMEM((dc // LANES, n2 * DFT_PITCH, LANES), F32)],
        compiler_params=_cparams("parallel", "parallel"),
        name="dft_stage2",
    )(br, bi, m2)


def _dft_tables(l, scale):
    n2 = DFT_N2
    n1 = l // n2

    def cs(num, den):
        ang = (2.0 * np.pi / den) * (num % den).astype(np.float64)
        return np.cos(ang), np.sin(ang)

    k1 = np.arange(n1)
    c1, s1 = cs(np.outer(k1, k1), n1)
    l2 = np.arange(n2)
    ct, st = cs(np.outer(k1, l2), l)
    c2, s2 = cs(np.outer(l2, l2), n2)
    return c1, s1, ct, st, c2 * scale, s2 * scale


def _fourier_latent(pq, dc):
    b, l, _ = pq.shape
    n2 = DFT_N2
    n1 = l // n2
    c1, s1, ct, st, c2, s2 = _dft_tables(l, 1.0 / math.sqrt(l * HEAD_DIM))
    m1 = jnp.asarray(np.concatenate([c1, s1], axis=0), F32).astype(BF16)
    m2 = jnp.asarray(np.concatenate([c2, -s2], axis=1), F32).astype(BF16)
    lanes = lambda t: jnp.asarray(np.broadcast_to(t.T[:, :, None], (n2, n1, LANES)), F32)
    br, bi = _dft_stage1_call(pq.reshape(b, n1, n2, 2 * dc), m1, lanes(ct), lanes(st))
    y = _dft_stage2_call(br, bi, m2)
    return y.reshape(b, l, dc)


def _fourier_direct(pq, dc):
    b, l, _ = pq.shape
    k = np.arange(l)
    ang = (2.0 * np.pi / l) * (np.outer(k, k) % l).astype(np.float64)
    scale = 1.0 / math.sqrt(l * HEAD_DIM)
    m = jnp.asarray(np.concatenate([np.cos(ang) * scale, -np.sin(ang) * scale], axis=1), F32).astype(BF16)
    return _dft_direct_call(pq, m)


MIX_STAGES = 5


def _mix_rows(rows, out_ref, gate_ref, uv_ref, hf_ref, hb_ref, yc_ref, x_ref, mod_ref, seg_ref, ws_ref, sgb_ref,
              gmix_ref, wout_ref, g_ref, b_ref, *, alpha, d_a, d_b):
    seg = seg_ref[...]
    inv = 1.0 / HEAD_DIM

    def seg_mean(hi, lo):
        return (_dot(hi, seg) + _dot(lo, seg)) * inv

    def rms(t, g):
        return t * lax.rsqrt(jnp.mean(t * t, axis=-1, keepdims=True) + LN_EPS) * g

    gm = gmix_ref[...]
    z = _gelu(uv_ref[rows, :].astype(F32))
    u = z[:, :d_b]
    v = z[:, d_b:]
    v_hi, v_lo = _split_bf16(v)
    yield
    dv = v - seg_mean(v_hi, v_lo)
    q_hi, q_lo = _split_bf16(dv * dv)
    ya = rms(_gelu(gate_ref[rows, :].astype(F32)) * (hf_ref[rows, :].astype(F32) + hb_ref[rows, :].astype(F32)),
             gm[:, :d_a])
    yield
    vn = dv * lax.rsqrt(seg_mean(q_hi, q_lo) + LN_EPS)
    lane = lax.broadcasted_iota(jnp.int32, (CHUNK, d_b), 1)
    heads = d_b // HEAD_DIM
    rhs = jnp.concatenate(
        [jnp.where((lane >= hh * HEAD_DIM) & (lane < (hh + 1) * HEAD_DIM), vn, 0.0) for hh in range(heads)],
        axis=0).astype(BF16)
    yc = rms(yc_ref[rows, :].astype(F32), gm[:, d_a + d_b:])
    yield
    yb = rms(u * (_dot(ws_ref[...], rhs) + sgb_ref[...]), gm[:, d_a:d_a + d_b])
    y = jnp.concatenate([ya, yb, yc], axis=-1).astype(BF16)
    yield
    o = _dot(y, wout_ref[...])
    m = mod_ref[...]
    out_ref[rows, :] = _ln(alpha * x_ref[rows, :] + m[2:3, :] * o) * g_ref[...] + b_ref[...]
    yield


def _ffn_body(x, m, wup_ref, wdn_ref, g_ref, b_ref, res_ref, between, *, alpha):
    res_ref[...] = x
    h = (_ln(x) * (1.0 + m[4:5, :]) + m[3:4, :]).astype(BF16)
    d_ff = wdn_ref.shape[0]
    for j in range(d_ff // FF_CHUNK):
        cols = pl.ds(j * FF_CHUNK, FF_CHUNK)
        g = _dot(h, wup_ref[:, cols])
        up = _dot(h, wup_ref[:, pl.ds(d_ff + j * FF_CHUNK, FF_CHUNK)])
        act = (g * jax.nn.sigmoid(g) * up).astype(BF16)
        part = _dot(act, wdn_ref[cols, :])
        acc = part if j == 0 else acc + part
        between(j)
    return _ln(alpha * res_ref[...] + m[5:6, :] * acc) * g_ref[...] + b_ref[...]


def _mixffn_kernel(gate_ref, uv_ref, hf_ref, hb_ref, yc_ref, x_ref, modm_ref, modf_ref, seg_ref, ws_ref, sgb_ref,
                   gmix_ref, wout_ref, g1_ref, b1_ref, wup_ref, wdn_ref, g2_ref, b2_ref,
                   o_ref, x1_ref, res_ref, *, alpha, d_a, d_b):
    @pl.when(pl.program_id(0) == 0)
    def _():
        x1_ref[...] = jnp.zeros_like(x1_ref)

    nchunk = x_ref.shape[0] // CHUNK
    nff = wdn_ref.shape[0] // FF_CHUNK
    todo = []
    for k0 in range(0, nchunk, 2):
        gens = [_mix_rows(pl.ds(c * CHUNK, CHUNK), x1_ref, gate_ref, uv_ref, hf_ref, hb_ref, yc_ref, x_ref,
                          modm_ref, seg_ref, ws_ref, sgb_ref, gmix_ref, wout_ref, g1_ref, b1_ref,
                          alpha=alpha, d_a=d_a, d_b=d_b) for c in range(k0, min(k0 + 2, nchunk))]
        todo += [g for _ in range(MIX_STAGES) for g in gens]
    per_gap = -(-len(todo) // (nff - 1))

    def between(j):
        count = len(todo) if j == nff - 1 else per_gap
        for _ in range(min(count, len(todo))):
            next(todo.pop(0))

    o_ref[...] = _ffn_body(x1_ref[...], modf_ref[...], wup_ref, wdn_ref, g2_ref, b2_ref, res_ref, between,
                           alpha=alpha)


def _mixffn_call(p_main, hf, hb, yc, x, mod, seg, ws_cat, sgb, g_mix, w_out, g1, b1, w_up, w_down, g2, b2,
                 *, layer, alpha, tile):
    b, l, d = x.shape
    d_a = hf.shape[-1]
    d_c = yc.shape[-1]
    d_b = d_c
    n = l // tile
    total = b * n

    def mixed(j):
        def index(s):
            t = jnp.minimum(s, total - 1)
            return (t // n, t % n, j)
        return index

    def fed(s):
        t = jnp.maximum(s - 1, 0)
        return (t // n, t % n, 0)

    vec = lambda shape: pl.BlockSpec(shape, lambda s: (0,) * len(shape))
    res = lambda shape: pl.BlockSpec((None,) + shape, lambda s: (layer, 0, 0), pipeline_mode=pl.Buffered(1))
    return pl.pallas_call(
        functools.partial(_mixffn_kernel, alpha=alpha, d_a=d_a, d_b=d_b),
        grid=(total + 1,),
        in_specs=[
            pl.BlockSpec((None, tile, d_a), mixed(1)),
            pl.BlockSpec((None, tile, 2 * d_b), mixed(2)),
            pl.BlockSpec((None, tile, d_a), mixed(0)),
            pl.BlockSpec((None, tile, d_a), mixed(0)),
            pl.BlockSpec((None, tile, d_c), mixed(0)),
            pl.BlockSpec((None, tile, d), mixed(0)),
            pl.BlockSpec((None, N_MOD, d), lambda s: (jnp.minimum(s, total - 1) // n, 0, 0)),
            pl.BlockSpec((None, N_MOD, d), lambda s: (jnp.maximum(s - 1, 0) // n, 0, 0)),
            vec(seg.shape), vec(ws_cat.shape), vec(sgb.shape), vec((1, d)),
            pl.BlockSpec((None,) + w_out.shape[1:], lambda s: (layer, 0, 0)),
            vec((1, d)), vec((1, d)),
            res(w_up.shape[1:]), res(w_down.shape[1:]), vec((1, d)), vec((1, d)),
        ],
        out_specs=pl.BlockSpec((None, tile, d), fed),
        out_shape=jax.ShapeDtypeStruct((b, l, d), F32),
        scratch_shapes=[pltpu.VMEM((tile, d), F32)] * 2,
        compiler_params=_cparams("arbitrary"),
        name="mix_ffn",
    )(p_main, p_main, hf, hb, yc, x, mod, mod, seg, ws_cat, sgb, g_mix, w_out, g1, b1, w_up, w_down, g2, b2)


def _pos_tables(rows, dim):
    quarter = dim // 4
    freqs = POS_BASE ** (-jnp.arange(quarter, dtype=F32) / quarter)

    def enc(p):
        ang = p[:, None] * freqs[None, :]
        return jnp.concatenate([jnp.sin(ang), jnp.cos(ang)], -1)

    return enc(jnp.arange(rows, dtype=F32)), enc(jnp.arange(GRID_W, dtype=F32))


def _block_diag(blocks):
    n, r, c = blocks.shape
    eye = jnp.eye(n, dtype=blocks.dtype)
    return (eye[:, None, :, None] * blocks[:, :, None, :]).reshape(n * r, n * c)


def _gate_weights(wa, wx):
    heads = wa.shape[0]
    pa = jnp.stack([_block_diag(wa[2 * j:2 * j + 2]) for j in range(heads // 2)])
    px = jnp.stack([_block_diag(wx[2 * j:2 * j + 2]) for j in range(heads // 2)])
    return jnp.concatenate([pa, px], axis=-1).astype(BF16)


def kernel(x, c, ctx, c_ctx, w_mod, b_mod, w_in, conv_w, conv_b, lru_wa, lru_ba, lru_wx, lru_bx, lru_lam,
           sg_ws, sg_b, fourier_w, g_mix, w_out, ln1_g, ln1_b, w_up, w_down, ln2_g, ln2_b):
    bsz, seq, d = x.shape
    ctx_len = ctx.shape[1]
    depth = w_mod.shape[0]
    d_a = conv_w.shape[-1]
    d_c = fourier_w.shape[1] * fourier_w.shape[2]
    alpha = (2 * depth) ** 0.25
    lat_tile = min(512, seq)
    inproj_tile = min(2 * INPROJ_ROWS, seq)
    ctx_tile = min(256, ctx_len)

    cc = jnp.zeros((SUBLANES, d), F32).at[:bsz].set(c).at[bsz].set(c_ctx)
    mod = _mod_call(cc, w_mod, b_mod).reshape(depth, SUBLANES, N_MOD, d)

    pos = _pos_tables(seq // GRID_W, d)
    k = np.arange(HEAD_DIM)
    ang = (2.0 * np.pi / HEAD_DIM) * (np.outer(k, k) % HEAD_DIM).astype(np.float64)
    groups_c = d_c // HEAD_DIM
    cc_bd = jnp.asarray(np.kron(np.eye(groups_c), np.cos(ang)), F32)
    sc_bd = jnp.asarray(np.kron(np.eye(groups_c), np.sin(ang)), F32)
    seg = jnp.asarray(np.kron(np.eye(groups_c), np.ones((HEAD_DIM, HEAD_DIM))), BF16)
    zeros_state = jnp.zeros((bsz, SUBLANES, d_a), F32)
    w_in_b, w_out_b, w_up_b, w_dn_b = (w.astype(BF16) for w in (w_in, w_out, w_up, w_down))

    xl, xc = x, ctx
    for l in range(depth):
        last = l == depth - 1
        mod_l = mod[l, :bsz]
        mod_c = jnp.broadcast_to(mod[l, bsz][None], (bsz, N_MOD, d))
        wcs = _fold_call(cc_bd, sc_bd, _block_diag(fourier_w[l])).astype(BF16)
        cw, cb = 0.5 * conv_w[l], 0.5 * conv_b[l].reshape(1, d_a)
        lru = [(cw, cb, _gate_weights(lru_wa[l, dd], lru_wx[l, dd]), 0.5 * lru_ba[l, dd].reshape(1, d_a),
                0.5 * lru_bx[l, dd].reshape(1, d_a), lru_lam[l, dd].reshape(1, d_a)) for dd in range(2)]
        ws_cat = jnp.concatenate([sg_ws[l, hh] for hh in range(sg_ws.shape[1])], axis=1).astype(BF16)
        sgb = jnp.repeat(jnp.transpose(sg_b[l]), HEAD_DIM, axis=1)
        g1, b1 = ln1_g[l].reshape(1, d), ln1_b[l].reshape(1, d)
        g2, b2 = ln2_g[l].reshape(1, d), ln2_b[l].reshape(1, d)
        gm = g_mix[l].reshape(1, d)

        def mix_ffn(p_main, hf, hb, yc, xs, mods, tile):
            return _mixffn_call(p_main, hf, hb, yc, xs, mods, seg, ws_cat, sgb, gm, w_out_b, g1, b1,
                                w_up_b, w_dn_b, g2, b2, layer=l, alpha=alpha, tile=tile)

        pc_main, pc_pq = _inproj_call(xc, None, mod_c, w_in_b, l, wcs, ctx_tile)
        hf_c, sf_c = _lru_call(pc_main, *lru[0], zeros_state, reverse=False, tile=ctx_tile)
        hb_c, sb_c = _lru_call(pc_main, *lru[1], zeros_state, reverse=True, tile=ctx_tile)
        if not last:
            flat = lambda a: a.reshape(1, bsz * ctx_len, a.shape[-1])
            xc = mix_ffn(flat(pc_main), flat(hf_c), flat(hb_c), flat(_fourier_direct(pc_pq, d_c)), flat(xc),
                         mod_c[:1], math.gcd(bsz * ctx_len, lat_tile)).reshape(bsz, ctx_len, d)

        if l == 0:
            xl, pl_main, pl_pq = _inproj_call(xl, pos, mod_l, w_in_b, l, wcs, inproj_tile)
        else:
            pl_main, pl_pq = _inproj_call(xl, None, mod_l, w_in_b, l, wcs, min(2 * inproj_tile, seq))
        hf, _ = _lru_call(pl_main, *lru[0], sf_c, reverse=False, tile=lat_tile)
        hb, _ = _lru_call(pl_main, *lru[1], sb_c, reverse=True, tile=lat_tile)
        if seq > 2 * DFT_N2:
            yc = _fourier_latent(pl_pq, d_c)
        else:
            yc = _fourier_direct(pl_pq, d_c)
        xl = mix_ffn(pl_main, hf, hb, yc, xl, mod_l, lat_tile)
    return xl
```

```python
import functools
import math

import jax
import jax.numpy as jnp
import numpy as np
from jax import lax
from jax.experimental import pallas as pl
from jax.experimental.pallas import tpu as pltpu

HEAD_DIM = 64
CHUNK = 128
GRID_W = 64
RG_C = 8.0
LN_EPS = 1e-6
POS_BASE = 10000.0
N_MOD = 6
SUBLANES = 8
LANES = 128
HALO = 2 * SUBLANES
FF_CHUNK = 256
DFT_N2 = 128
DFT_TILE = HALO
DFT_PITCH = DFT_TILE + SUBLANES
INPROJ_ROWS = 512
VMEM_LIMIT = 56 * 1024 * 1024

F32 = jnp.float32
BF16 = jnp.bfloat16


def _dot(a, b):
    return jnp.dot(a, b, preferred_element_type=F32)


def _ln(x):
    mu = jnp.mean(x, axis=-1, keepdims=True)
    d = x - mu
    var = jnp.mean(d * d, axis=-1, keepdims=True)
    return d * lax.rsqrt(var + LN_EPS)


def _gelu(x):
    return 0.5 * x * (1.0 + lax.erf(x * (1.0 / math.sqrt(2.0))))


def _cparams(*sem):
    return pltpu.CompilerParams(dimension_semantics=sem, vmem_limit_bytes=VMEM_LIMIT)


def _split_bf16(t):
    hi = t.astype(BF16)
    return hi, (t - hi.astype(F32)).astype(BF16)


def _mod_kernel(c_ref, w_ref, b_ref, o_ref):
    c = c_ref[...]
    s = (c * jax.nn.sigmoid(c)).astype(BF16)
    o_ref[...] = _dot(s, w_ref[...].astype(BF16)) + b_ref[...]


def _mod_call(cc, w_mod, b_mod):
    depth, d, nm = w_mod.shape
    tn = 1536
    return pl.pallas_call(
        _mod_kernel,
        grid=(depth, nm // tn),
        in_specs=[
            pl.BlockSpec((SUBLANES, d), lambda l, j: (0, 0)),
            pl.BlockSpec((None, d, tn), lambda l, j: (l, 0, j)),
            pl.BlockSpec((None, 1, tn), lambda l, j: (l, 0, j)),
        ],
        out_specs=pl.BlockSpec((None, SUBLANES, tn), lambda l, j: (l, 0, j)),
        out_shape=jax.ShapeDtypeStruct((depth, SUBLANES, nm), F32),
        compiler_params=_cparams("parallel", "parallel"),
        name="mod_vectors",
    )(cc, w_mod, b_mod.reshape(depth, 1, nm))


def _dot_3pass(a, b):
    a_hi, a_lo = _split_bf16(a)
    b_hi, b_lo = _split_bf16(b)
    return _dot(a_hi, b_hi) + (_dot(a_hi, b_lo) + _dot(a_lo, b_hi))


def _fold_kernel(cc_ref, sc_ref, wf_ref, o_ref):
    wf = wf_ref[...]
    n = wf.shape[0]
    o_ref[:, :n] = _dot_3pass(cc_ref[...], wf)
    o_ref[:, n:] = _dot_3pass(sc_ref[...], wf)


def _fold_call(cc_bd, sc_bd, wf_bd):
    n = wf_bd.shape[0]
    return pl.pallas_call(
        _fold_kernel,
        out_shape=jax.ShapeDtypeStruct((n, 2 * n), F32),
        name="fourier_weight_fold",
    )(cc_bd, sc_bd, wf_bd)


def _inproj_kernel(*refs, add_pos, d_main):
    if add_pos:
        x_ref, prow_ref, pcol_ref, mod_ref, w_ref, wcs_ref, xo_ref, pm_ref, pq_ref = refs
    else:
        x_ref, mod_ref, w_ref, wcs_ref, pm_ref, pq_ref = refs
    tile, d = x_ref.shape
    m = mod_ref[...]
    sub = min(tile, INPROJ_ROWS)

    def normed(k):
        rows = pl.ds(k * sub, sub)
        if add_pos:
            half = d // 2
            nrow = sub // GRID_W
            prow = prow_ref[pl.ds(k * nrow, nrow), :]
            xr = x_ref[rows, :half].reshape(nrow, GRID_W, half) + prow[:, None, :]
            xq = x_ref[rows, half:].reshape(nrow, GRID_W, half) + pcol_ref[...][None, :, :]
            x = jnp.concatenate([xr.reshape(sub, half), xq.reshape(sub, half)], axis=-1)
            xo_ref[rows, :] = x
        else:
            x = x_ref[rows, :]
        return (_ln(x) * (1.0 + m[1:2, :]) + m[0:1, :]).astype(BF16)

    nsub = tile // sub
    h = normed(0)
    for k in range(nsub):
        rows = pl.ds(k * sub, sub)
        p = _dot(h, w_ref[...])
        if k + 1 < nsub:
            h = normed(k + 1)
        pm_ref[rows, :] = p[:, :d_main].astype(pm_ref.dtype)
        pq_ref[rows, :] = _dot(p[:, d_main:].astype(BF16), wcs_ref[...]).astype(pq_ref.dtype)


def _inproj_call(x, pos, mod, w_in, layer, wcs, tile):
    b, l, d = x.shape
    d_in = w_in.shape[2]
    d_c = wcs.shape[0]
    d_main = d_in - d_c
    add_pos = pos is not None
    tok = lambda bi, i: (bi, i, 0)
    in_specs = [pl.BlockSpec((None, tile, d), tok)]
    args = [x]
    if add_pos:
        prow, pcol = pos
        in_specs.append(pl.BlockSpec((tile // GRID_W, d // 2), lambda bi, i: (i, 0)))
        in_specs.append(pl.BlockSpec((GRID_W, d // 2), lambda bi, i: (0, 0)))
        args += [prow, pcol]
    in_specs += [
        pl.BlockSpec((None, N_MOD, d), lambda bi, i: (bi, 0, 0)),
        pl.BlockSpec((None, d, d_in), lambda bi, i: (layer, 0, 0)),
        pl.BlockSpec((d_c, 2 * d_c), lambda bi, i: (0, 0)),
    ]
    args += [mod, w_in, wcs]
    out_specs = [pl.BlockSpec((None, tile, d_main), tok), pl.BlockSpec((None, tile, 2 * d_c), tok)]
    out_shape = [jax.ShapeDtypeStruct((b, l, d_main), BF16), jax.ShapeDtypeStruct((b, l, 2 * d_c), BF16)]
    if add_pos:
        out_specs.insert(0, pl.BlockSpec((None, tile, d), tok))
        out_shape.insert(0, jax.ShapeDtypeStruct((b, l, d), F32))
    return pl.pallas_call(
        functools.partial(_inproj_kernel, add_pos=add_pos, d_main=d_main),
        grid=(b, l // tile),
        in_specs=in_specs,
        out_specs=out_specs,
        out_shape=out_shape,
        compiler_params=_cparams("parallel", "parallel"),
        name="in_proj",
    )(*args)


def _scan_scratch(tile, d_a):
    G = tile // SUBLANES
    pitch = G + SUBLANES
    return [
        pltpu.VMEM((d_a // LANES, SUBLANES * pitch, LANES), F32),
        pltpu.VMEM((G + 3, SUBLANES, d_a), F32),
        pltpu.VMEM((G, SUBLANES, d_a), F32),
        pltpu.VMEM((G, SUBLANES, d_a), F32),
        pltpu.VMEM((G, SUBLANES, d_a), F32),
        pltpu.VMEM((G, SUBLANES, d_a), F32),
        pltpu.VMEM((SUBLANES, d_a), F32),
    ]


def _scan_tile(x_ref, prev_ref, next_ref, cw_ref, cb_ref, wg_ref, ba_ref, bx_ref, lam_ref,
               out_ref, slab_ref, xt_ref, gr_ref, gi_ref, hl_ref, al_ref, carry_ref,
               *, reverse, seq_first, seq_last):
    tile, d_a = x_ref.shape
    G = tile // SUBLANES
    nq = d_a // LANES
    pitch = slab_ref.shape[1] // SUBLANES
    sub = lax.broadcasted_iota(jnp.int32, (SUBLANES, d_a), 0)
    bc = lambda row: jnp.broadcast_to(row, (SUBLANES, d_a))

    xv = x_ref[...].astype(F32)
    for s in range(SUBLANES):
        for q in range(nq):
            slab_ref[q, s * pitch:s * pitch + G, :] = xv[s * G:(s + 1) * G, q * LANES:(q + 1) * LANES]

    for g in range(G):
        for q in range(nq):
            xt_ref[g + 2, :, q * LANES:(q + 1) * LANES] = slab_ref[q, pl.ds(g, SUBLANES, stride=pitch), :]
    prev = jnp.where(seq_first, 0.0, prev_ref[...].astype(F32))
    nxt = jnp.where(seq_last, 0.0, next_ref[...].astype(F32))
    hrows = prev_ref.shape[0]
    xt_ref[0] = jnp.where(sub == 0, bc(prev[hrows - 2:hrows - 1, :]), pltpu.roll(xt_ref[G], 1, 0))
    xt_ref[1] = jnp.where(sub == 0, bc(prev[hrows - 1:hrows, :]), pltpu.roll(xt_ref[G + 1], 1, 0))
    xt_ref[G + 2] = jnp.where(sub == SUBLANES - 1, bc(nxt[0:1, :]), pltpu.roll(xt_ref[2], SUBLANES - 1, 0))
    cw = cw_ref[...]
    xh = (cb_ref[...][None] + cw[0:1, :][None] * xt_ref[0:G] + cw[1:2, :][None] * xt_ref[1:G + 1]
          + cw[2:3, :][None] * xt_ref[2:G + 2] + cw[3:4, :][None] * xt_ref[3:G + 3])
    xt_ref[0:G] = xh
    xhb = xh.reshape(tile, d_a).astype(BF16)

    pair = 2 * HEAD_DIM
    for j in range(d_a // pair):
        g2 = _dot(xhb[:, j * pair:(j + 1) * pair], wg_ref[j])
        gr_ref[:, :, j * pair:(j + 1) * pair] = g2[:, :pair].reshape(G, SUBLANES, pair)
        gi_ref[:, :, j * pair:(j + 1) * pair] = g2[:, pair:].reshape(G, SUBLANES, pair)
    lam = lam_ref[...]
    neg = -lam
    softplus = jnp.maximum(neg, 0.0) + jnp.log1p(jnp.exp(-jnp.abs(neg)))
    k2 = bc((-0.5 * RG_C / math.log(2.0)) * softplus)
    hba = bc(ba_ref[...])
    hbx = bc(bx_ref[...])

    h = jnp.zeros((SUBLANES, d_a), F32)
    acum = h + 1.0
    for t in range(G):
        g = (G - 1 - t) if reverse else t
        t_r = jnp.tanh(gr_ref[g] + hba)
        t_i = jnp.tanh(gi_ref[g] + hbx)
        l2 = k2 + k2 * t_r
        a = jnp.exp2(l2)
        w = jnp.tanh(l2 * (-math.log(2.0))) * (1.0 + a * a)
        xg = xt_ref[g]
        root = jnp.where(w > 0.0, w * lax.rsqrt(w), 0.0)
        h = a * h + root * (xg * t_i + xg)
        acum = acum * a
        hl_ref[g] = h
        al_ref[g] = acum

    first = SUBLANES - 1 if reverse else 0
    shift = SUBLANES - 1 if reverse else 1
    c_in = carry_ref[...]
    c = c_in
    for _ in range(SUBLANES - 1):
        c = jnp.where(sub == first, c_in, pltpu.roll(h + acum * c, shift, 0))
    last = SUBLANES - 1 - first
    carry_ref[...] = bc((h + acum * c)[last:last + 1, :])

    for g in range(G):
        hfull = hl_ref[g] + al_ref[g] * c
        for q in range(nq):
            slab_ref[q, pl.ds(g, SUBLANES, stride=pitch), :] = hfull[:, q * LANES:(q + 1) * LANES]
    for s in range(SUBLANES):
        for q in range(nq):
            out_ref[s * G:(s + 1) * G, q * LANES:(q + 1) * LANES] = (
                slab_ref[q, s * pitch:s * pitch + G, :].astype(out_ref.dtype))


def _lru_kernel(x_ref, prev_ref, next_ref, cw_ref, cb_ref, wg_ref, ba_ref, bx_ref, lam_ref, h0_ref,
                h_ref, hfin_ref, *scratch, reverse):
    i = pl.program_id(1)
    n = pl.num_programs(1)
    li = (n - 1 - i) if reverse else i

    @pl.when(i == 0)
    def _():
        scratch[-1][...] = h0_ref[...]

    _scan_tile(x_ref, prev_ref, next_ref, cw_ref, cb_ref, wg_ref, ba_ref, bx_ref, lam_ref,
               h_ref, *scratch, reverse=reverse, seq_first=li == 0, seq_last=li == n - 1)
    hfin_ref[...] = scratch[-1][...]


def _lru_call(p_main, conv_w_half, conv_b_half, wg, ba_half, bx_half, lam, h0, *, reverse, tile):
    b, l, _ = p_main.shape
    d_a = conv_w_half.shape[-1]
    n = l // tile
    per = tile // HALO
    nblk = l // HALO
    pos = (lambda i: n - 1 - i) if reverse else (lambda i: i)
    vec = lambda shape: pl.BlockSpec(shape, lambda bi, i: (0,) * len(shape))
    return pl.pallas_call(
        functools.partial(_lru_kernel, reverse=reverse),
        grid=(b, n),
        in_specs=[
            pl.BlockSpec((None, tile, d_a), lambda bi, i: (bi, pos(i), 0)),
            pl.BlockSpec((None, HALO, d_a), lambda bi, i: (bi, jnp.maximum(pos(i) * per - 1, 0), 0)),
            pl.BlockSpec((None, HALO, d_a), lambda bi, i: (bi, jnp.minimum((pos(i) + 1) * per, nblk - 1), 0)),
            vec((4, d_a)), vec((1, d_a)), vec(wg.shape), vec((1, d_a)), vec((1, d_a)), vec((1, d_a)),
            pl.BlockSpec((None, SUBLANES, d_a), lambda bi, i: (bi, 0, 0)),
        ],
        out_specs=[
            pl.BlockSpec((None, tile, d_a), lambda bi, i: (bi, pos(i), 0)),
            pl.BlockSpec((None, SUBLANES, d_a), lambda bi, i: (bi, 0, 0)),
        ],
        out_shape=[jax.ShapeDtypeStruct((b, l, d_a), BF16), jax.ShapeDtypeStruct((b, SUBLANES, d_a), F32)],
        scratch_shapes=_scan_scratch(tile, d_a),
        compiler_params=_cparams("arbitrary", "arbitrary"),
        name="lru_scan_bwd" if reverse else "lru_scan_fwd",
    )(p_main, p_main, p_main, conv_w_half, conv_b_half, wg, ba_half, bx_half, lam, h0)


def _dft_direct_kernel(pq_ref, m_ref, y_ref):
    pq = pq_ref[...]
    dc = pq.shape[-1] // 2
    rhs = jnp.concatenate([pq[:, :dc], pq[:, dc:]], axis=0).astype(BF16)
    y_ref[...] = _dot(m_ref[...], rhs).astype(y_ref.dtype)


def _dft_direct_call(pq, m):
    b, l, dc2 = pq.shape
    return pl.pallas_call(
        _dft_direct_kernel,
        grid=(b,),
        in_specs=[pl.BlockSpec((None, l, dc2), lambda bi: (bi, 0, 0)),
                  pl.BlockSpec((l, 2 * l), lambda bi: (0, 0))],
        out_specs=pl.BlockSpec((None, l, dc2 // 2), lambda bi: (bi, 0, 0)),
        out_shape=jax.ShapeDtypeStruct((b, l, dc2 // 2), BF16),
        compiler_params=_cparams("parallel"),
        name="dft_direct",
    )(pq, m)


def _slab_rows(a, width):
    return pl.ds(a * DFT_PITCH, width)


def _dft_stage1_kernel(x_ref, m_ref, ct_ref, st_ref, br_ref, bi_ref, xs_ref, os_ref):
    n1, tl, w = x_ref.shape
    dc = w // 2
    for a in range(n1):
        xa = x_ref[a].astype(F32)
        for q in range(w // LANES):
            xs_ref[q, _slab_rows(a, tl), :] = xa[:, q * LANES:(q + 1) * LANES]
    m = m_ref[...]
    for t in range(tl):
        every = pl.ds(t, n1, stride=DFT_PITCH)
        x = jnp.concatenate([xs_ref[q, every, :] for q in range(w // LANES)], axis=-1).astype(BF16)
        cs = _dot(m, x)
        ar = cs[:n1, :dc] - cs[n1:, dc:]
        ai = cs[:n1, dc:] + cs[n1:, :dc]
        ct = jnp.concatenate([ct_ref[t]] * (dc // LANES), axis=-1)
        st = jnp.concatenate([st_ref[t]] * (dc // LANES), axis=-1)
        br = ct * ar - st * ai
        bi = ct * ai + st * ar
        for q in range(dc // LANES):
            os_ref[0, q, every, :] = br[:, q * LANES:(q + 1) * LANES]
            os_ref[1, q, every, :] = bi[:, q * LANES:(q + 1) * LANES]
    for a in range(n1):
        for part, o_ref in enumerate((br_ref, bi_ref)):
            o_ref[a] = jnp.concatenate([os_ref[part, q, _slab_rows(a, tl), :] for q in range(dc // LANES)],
                                       axis=-1).astype(o_ref.dtype)


def _dft_stage1_call(x4, m1, ct, st):
    b, n1, n2, w = x4.shape
    dc = w // 2
    tl = min(DFT_TILE, n2)
    out_blk = pl.BlockSpec((None, n1, tl, dc), lambda j, bi: (bi, 0, j, 0))
    tab_blk = pl.BlockSpec((tl, n1, LANES), lambda j, bi: (j, 0, 0))
    return pl.pallas_call(
        _dft_stage1_kernel,
        grid=(n2 // tl, b),
        in_specs=[
            pl.BlockSpec((None, n1, tl, w), lambda j, bi: (bi, 0, j, 0)),
            pl.BlockSpec((2 * n1, n1), lambda j, bi: (0, 0)),
            tab_blk, tab_blk,
        ],
        out_specs=[out_blk, out_blk],
        out_shape=[jax.ShapeDtypeStruct((b, n1, n2, dc), BF16)] * 2,
        scratch_shapes=[pltpu.VMEM((w // LANES, n1 * DFT_PITCH, LANES), F32),
                        pltpu.VMEM((2, dc // LANES, n1 * DFT_PITCH, LANES), F32)],
        compiler_params=_cparams("parallel", "parallel"),
        name="dft_stage1",
    )(x4, m1, ct, st)


def _dft_stage2_kernel(br_ref, bi_ref, m_ref, y_ref, ys_ref):
    tk, n2, dc = br_ref.shape
    m = m_ref[...]
    for a in range(tk):
        y = _dot(m, jnp.concatenate([br_ref[a], bi_ref[a]], axis=0).astype(BF16))
        for q in range(dc // LANES):
            ys_ref[q, pl.ds(a, n2, stride=DFT_PITCH), :] = y[:, q * LANES:(q + 1) * LANES]
    for k2 in range(n2):
        y_ref[k2] = jnp.concatenate([ys_ref[q, _slab_rows(k2, tk), :] for q in range(dc // LANES)],
                                    axis=-1).astype(y_ref.dtype)


def _dft_stage2_call(br, bi, m2):
    b, n1, n2, dc = br.shape
    tk = min(DFT_TILE, n1)
    blk = pl.BlockSpec((None, tk, n2, dc), lambda bi_, k: (bi_, k, 0, 0))
    return pl.pallas_call(
        _dft_stage2_kernel,
        grid=(b, n1 // tk),
        in_specs=[blk, blk, pl.BlockSpec((n2, 2 * n2), lambda bi_, k: (0, 0))],
        out_specs=pl.BlockSpec((None, n2, tk, dc), lambda bi_, k: (bi_, 0, k, 0)),
        out_shape=jax.ShapeDtypeStruct((b, n2, n1, dc), BF16),
        scratch_shapes=[pltpu.VMEM((dc // LANES, n2 * DFT_PITCH, LANES), F32)],
        compiler_params=_cparams("parallel", "parallel"),
        name="dft_stage2",
    )(br, bi, m2)


def _dft_tables(l, scale):
    n2 = DFT_N2
    n1 = l // n2

    def cs(num, den):
        ang = (2.0 * np.pi / den) * (num % den).astype(np.float64)
        return np.cos(ang), np.sin(ang)

    k1 = np.arange(n1)
    c1, s1 = cs(np.outer(k1, k1), n1)
    l2 = np.arange(n2)
    ct, st = cs(np.outer(k1, l2), l)
    c2, s2 = cs(np.outer(l2, l2), n2)
    return c1, s1, ct, st, c2 * scale, s2 * scale


def _fourier_latent(pq, dc):
    b, l, _ = pq.shape
    n2 = DFT_N2
    n1 = l // n2
    c1, s1, ct, st, c2, s2 = _dft_tables(l, 1.0 / math.sqrt(l * HEAD_DIM))
    m1 = jnp.asarray(np.concatenate([c1, s1], axis=0), F32).astype(BF16)
    m2 = jnp.asarray(np.concatenate([c2, -s2], axis=1), F32).astype(BF16)
    lanes = lambda t: jnp.asarray(np.broadcast_to(t.T[:, :, None], (n2, n1, LANES)), F32)
    br, bi = _dft_stage1_call(pq.reshape(b, n1, n2, 2 * dc), m1, lanes(ct), lanes(st))
    y = _dft_stage2_call(br, bi, m2)
    return y.reshape(b, l, dc)


def _fourier_direct(pq, dc):
    b, l, _ = pq.shape
    k = np.arange(l)
    ang = (2.0 * np.pi / l) * (np.outer(k, k) % l).astype(np.float64)
    scale = 1.0 / math.sqrt(l * HEAD_DIM)
    m = jnp.asarray(np.concatenate([np.cos(ang) * scale, -np.sin(ang) * scale], axis=1), F32).astype(BF16)
    return _dft_direct_call(pq, m)


MIX_STAGES = 5


def _mix_rows(rows, out_ref, gate_ref, uv_ref, hf_ref, hb_ref, yc_ref, x_ref, mod_ref, seg_ref, ws_ref, sgb_ref,
              gmix_ref, wout_ref, g_ref, b_ref, *, alpha, d_a, d_b):
    seg = seg_ref[...]
    inv = 1.0 / HEAD_DIM

    def seg_mean(hi, lo):
        return (_dot(hi, seg) + _dot(lo, seg)) * inv

    def rms(t, g):
        return t * lax.rsqrt(jnp.mean(t * t, axis=-1, keepdims=True) + LN_EPS) * g

    gm = gmix_ref[...]
    z = _gelu(uv_ref[rows, :].astype(F32))
    u = z[:, :d_b]
    v = z[:, d_b:]
    v_hi, v_lo = _split_bf16(v)
    yield
    dv = v - seg_mean(v_hi, v_lo)
    q_hi, q_lo = _split_bf16(dv * dv)
    ya = rms(_gelu(gate_ref[rows, :].astype(F32)) * (hf_ref[rows, :].astype(F32) + hb_ref[rows, :].astype(F32)),
             gm[:, :d_a])
    yield
    vn = dv * lax.rsqrt(seg_mean(q_hi, q_lo) + LN_EPS)
    lane = lax.broadcasted_iota(jnp.int32, (CHUNK, d_b), 1)
    heads = d_b // HEAD_DIM
    rhs = jnp.concatenate(
        [jnp.where((lane >= hh * HEAD_DIM) & (lane < (hh + 1) * HEAD_DIM), vn, 0.0) for hh in range(heads)],
        axis=0).astype(BF16)
    yc = rms(yc_ref[rows, :].astype(F32), gm[:, d_a + d_b:])
    yield
    yb = rms(u * (_dot(ws_ref[...], rhs) + sgb_ref[...]), gm[:, d_a:d_a + d_b])
    y = jnp.concatenate([ya, yb, yc], axis=-1).astype(BF16)
    yield
    o = _dot(y, wout_ref[...])
    m = mod_ref[...]
    out_ref[rows, :] = _ln(alpha * x_ref[rows, :] + m[2:3, :] * o) * g_ref[...] + b_ref[...]
    yield


def _ffn_body(x, m, wup_ref, wdn_ref, g_ref, b_ref, res_ref, between, *, alpha):
    res_ref[...] = x
    h = (_ln(x) * (1.0 + m[4:5, :]) + m[3:4, :]).astype(BF16)
    d_ff = wdn_ref.shape[0]
    for j in range(d_ff // FF_CHUNK):
        cols = pl.ds(j * FF_CHUNK, FF_CHUNK)
        g = _dot(h, wup_ref[:, cols])
        up = _dot(h, wup_ref[:, pl.ds(d_ff + j * FF_CHUNK, FF_CHUNK)])
        act = (g * jax.nn.sigmoid(g) * up).astype(BF16)
        part = _dot(act, wdn_ref[cols, :])
        acc = part if j == 0 else acc + part
        between(j)
    return _ln(alpha * res_ref[...] + m[5:6, :] * acc) * g_ref[...] + b_ref[...]


def _mixffn_kernel(gate_ref, uv_ref, hf_ref, hb_ref, yc_ref, x_ref, modm_ref, modf_ref, seg_ref, ws_ref, sgb_ref,
                   gmix_ref, wout_ref, g1_ref, b1_ref, wup_ref, wdn_ref, g2_ref, b2_ref,
                   o_ref, x1_ref, res_ref, *, alpha, d_a, d_b):
    step = pl.program_id(0)
    last = pl.num_programs(0) - 1
    nchunk = x_ref.shape[0] // CHUNK
    nff = wdn_ref.shape[0] // FF_CHUNK

    def mix_stages():
        todo = []
        for k0 in range(0, nchunk, 2):
            gens = [_mix_rows(pl.ds(c * CHUNK, CHUNK), x1_ref, gate_ref, uv_ref, hf_ref, hb_ref, yc_ref, x_ref,
                              modm_ref, seg_ref, ws_ref, sgb_ref, gmix_ref, wout_ref, g1_ref, b1_ref,
                              alpha=alpha, d_a=d_a, d_b=d_b) for c in range(k0, min(k0 + 2, nchunk))]
            todo += [g for _ in range(MIX_STAGES) for g in gens]
        return todo

    def ffn(between):
        o_ref[...] = _ffn_body(x1_ref[...], modf_ref[...], wup_ref, wdn_ref, g2_ref, b2_ref, res_ref, between,
                               alpha=alpha)

    @pl.when(step == 0)
    def _():
        for gen in mix_stages():
            next(gen)

    @pl.when((step > 0) & (step < last))
    def _():
        todo = mix_stages()
        per_gap = -(-len(todo) // (nff - 1))

        def between(j):
            count = len(todo) if j == nff - 1 else per_gap
            for _ in range(min(count, len(todo))):
                next(todo.pop(0))

        ffn(between)

    @pl.when(step == last)
    def _():
        ffn(lambda j: None)


def _mixffn_call(p_main, hf, hb, yc, x, mod, seg, ws_cat, sgb, g_mix, w_out, g1, b1, w_up, w_down, g2, b2,
                 *, layer, alpha, tile):
    b, l, d = x.shape
    d_a = hf.shape[-1]
    d_c = yc.shape[-1]
    d_b = d_c
    n = l // tile
    total = b * n

    def mixed(j):
        def index(s):
            t = jnp.minimum(s, total - 1)
            return (t // n, t % n, j)
        return index

    def fed(s):
        t = jnp.maximum(s - 1, 0)
        return (t // n, t % n, 0)

    vec = lambda shape: pl.BlockSpec(shape, lambda s: (0,) * len(shape))
    res = lambda shape: pl.BlockSpec((None,) + shape, lambda s: (layer, 0, 0), pipeline_mode=pl.Buffered(1))
    return pl.pallas_call(
        functools.partial(_mixffn_kernel, alpha=alpha, d_a=d_a, d_b=d_b),
        grid=(total + 1,),
        in_specs=[
            pl.BlockSpec((None, tile, d_a), mixed(1)),
            pl.BlockSpec((None, tile, 2 * d_b), mixed(2)),
            pl.BlockSpec((None, tile, d_a), mixed(0)),
            pl.BlockSpec((None, tile, d_a), mixed(0)),
            pl.BlockSpec((None, tile, d_c), mixed(0)),
            pl.BlockSpec((None, tile, d), mixed(0)),
            pl.BlockSpec((None, N_MOD, d), lambda s: (jnp.minimum(s, total - 1) // n, 0, 0)),
            pl.BlockSpec((None, N_MOD, d), lambda s: (jnp.maximum(s - 1, 0) // n, 0, 0)),
            vec(seg.shape), vec(ws_cat.shape), vec(sgb.shape), vec((1, d)),
            pl.BlockSpec((None,) + w_out.shape[1:], lambda s: (layer, 0, 0)),
            vec((1, d)), vec((1, d)),
            res(w_up.shape[1:]), res(w_down.shape[1:]), vec((1, d)), vec((1, d)),
        ],
        out_specs=pl.BlockSpec((None, tile, d), fed),
        out_shape=jax.ShapeDtypeStruct((b, l, d), F32),
        scratch_shapes=[pltpu.VMEM((tile, d), F32)] * 2,
        compiler_params=_cparams("arbitrary"),
        name="mix_ffn",
    )(p_main, p_main, hf, hb, yc, x, mod, mod, seg, ws_cat, sgb, g_mix, w_out, g1, b1, w_up, w_down, g2, b2)


def _pos_tables(rows, dim):
    quarter = dim // 4
    freqs = POS_BASE ** (-jnp.arange(quarter, dtype=F32) / quarter)

    def enc(p):
        ang = p[:, None] * freqs[None, :]
        return jnp.concatenate([jnp.sin(ang), jnp.cos(ang)], -1)

    return enc(jnp.arange(rows, dtype=F32)), enc(jnp.arange(GRID_W, dtype=F32))


def _block_diag(blocks):
    n, r, c = blocks.shape
    eye = jnp.eye(n, dtype=blocks.dtype)
    return (eye[:, None, :, None] * blocks[:, :, None, :]).reshape(n * r, n * c)


def _gate_weights(wa, wx):
    heads = wa.shape[0]
    pa = jnp.stack([_block_diag(wa[2 * j:2 * j + 2]) for j in range(heads // 2)])
    px = jnp.stack([_block_diag(wx[2 * j:2 * j + 2]) for j in range(heads // 2)])
    return jnp.concatenate([pa, px], axis=-1).astype(BF16)


def kernel(x, c, ctx, c_ctx, w_mod, b_mod, w_in, conv_w, conv_b, lru_wa, lru_ba, lru_wx, lru_bx, lru_lam,
           sg_ws, sg_b, fourier_w, g_mix, w_out, ln1_g, ln1_b, w_up, w_down, ln2_g, ln2_b):
    bsz, seq, d = x.shape
    ctx_len = ctx.shape[1]
    depth = w_mod.shape[0]
    d_a = conv_w.shape[-1]
    d_c = fourier_w.shape[1] * fourier_w.shape[2]
    alpha = (2 * depth) ** 0.25
    lat_tile = min(512, seq)
    inproj_tile = min(2 * INPROJ_ROWS, seq)
    ctx_tile = min(256, ctx_len)

    cc = jnp.zeros((SUBLANES, d), F32).at[:bsz].set(c).at[bsz].set(c_ctx)
    mod = _mod_call(cc, w_mod, b_mod).reshape(depth, SUBLANES, N_MOD, d)

    pos = _pos_tables(seq // GRID_W, d)
    k = np.arange(HEAD_DIM)
    ang = (2.0 * np.pi / HEAD_DIM) * (np.outer(k, k) % HEAD_DIM).astype(np.float64)
    groups_c = d_c // HEAD_DIM
    cc_bd = jnp.asarray(np.kron(np.eye(groups_c), np.cos(ang)), F32)
    sc_bd = jnp.asarray(np.kron(np.eye(groups_c), np.sin(ang)), F32)
    seg = jnp.asarray(np.kron(np.eye(groups_c), np.ones((HEAD_DIM, HEAD_DIM))), BF16)
    zeros_state = jnp.zeros((bsz, SUBLANES, d_a), F32)
    w_in_b, w_out_b, w_up_b, w_dn_b = (w.astype(BF16) for w in (w_in, w_out, w_up, w_down))

    xl, xc = x, ctx
    for l in range(depth):
        last = l == depth - 1
        mod_l = mod[l, :bsz]
        mod_c = jnp.broadcast_to(mod[l, bsz][None], (bsz, N_MOD, d))
        wcs = _fold_call(cc_bd, sc_bd, _block_diag(fourier_w[l])).astype(BF16)
        cw, cb = 0.5 * conv_w[l], 0.5 * conv_b[l].reshape(1, d_a)
        lru = [(cw, cb, _gate_weights(lru_wa[l, dd], lru_wx[l, dd]), 0.5 * lru_ba[l, dd].reshape(1, d_a),
                0.5 * lru_bx[l, dd].reshape(1, d_a), lru_lam[l, dd].reshape(1, d_a)) for dd in range(2)]
        ws_cat = jnp.concatenate([sg_ws[l, hh] for hh in range(sg_ws.shape[1])], axis=1).astype(BF16)
        sgb = jnp.repeat(jnp.transpose(sg_b[l]), HEAD_DIM, axis=1)
        g1, b1 = ln1_g[l].reshape(1, d), ln1_b[l].reshape(1, d)
        g2, b2 = ln2_g[l].reshape(1, d), ln2_b[l].reshape(1, d)
        gm = g_mix[l].reshape(1, d)

        def mix_ffn(p_main, hf, hb, yc, xs, mods, tile):
            return _mixffn_call(p_main, hf, hb, yc, xs, mods, seg, ws_cat, sgb, gm, w_out_b, g1, b1,
                                w_up_b, w_dn_b, g2, b2, layer=l, alpha=alpha, tile=tile)

        pc_main, pc_pq = _inproj_call(xc, None, mod_c, w_in_b, l, wcs, ctx_tile)
        hf_c, sf_c = _lru_call(pc_main, *lru[0], zeros_state, reverse=False, tile=ctx_tile)
        hb_c, sb_c = _lru_call(pc_main, *lru[1], zeros_state, reverse=True, tile=ctx_tile)
        if not last:
            flat = lambda a: a.reshape(1, bsz * ctx_len, a.shape[-1])
            xc = mix_ffn(flat(pc_main), flat(hf_c), flat(hb_c), flat(_fourier_direct(pc_pq, d_c)), flat(xc),
                         mod_c[:1], math.gcd(bsz * ctx_len, lat_tile)).reshape(bsz, ctx_len, d)

        if l == 0:
            xl, pl_main, pl_pq = _inproj_call(xl, pos, mod_l, w_in_b, l, wcs, inproj_tile)
        else:
            pl_main, pl_pq = _inproj_call(xl, None, mod_l, w_in_b, l, wcs, min(2 * inproj_tile, seq))
        hf, _ = _lru_call(pl_main, *lru[0], sf_c, reverse=False, tile=lat_tile)
        hb, _ = _lru_call(pl_main, *lru[1], sb_c, reverse=True, tile=lat_tile)
        if seq > 2 * DFT_N2:
            yc = _fourier_latent(pl_pq, d_c)
        else:
            yc = _fourier_direct(pl_pq, d_c)
        xl = mix_ffn(pl_main, hf, hb, yc, xl, mod_l, lat_tile)
    return xl
```

```python
import functools
import math

import jax
import jax.numpy as jnp
import numpy as np
from jax import lax
from jax.experimental import pallas as pl
from jax.experimental.pallas import tpu as pltpu

HEAD_DIM = 64
CHUNK = 128
GRID_W = 64
RG_C = 8.0
LN_EPS = 1e-6
POS_BASE = 10000.0
N_MOD = 6
SUBLANES = 8
LANES = 128
HALO = 2 * SUBLANES
FF_CHUNK = 256
DFT_N2 = 128
DFT_TILE = 2 * HALO
DFT_PITCH = DFT_TILE + SUBLANES
INPROJ_ROWS = 512
VMEM_LIMIT = 56 * 1024 * 1024

F32 = jnp.float32
BF16 = jnp.bfloat16


def _dot(a, b):
    return jnp.dot(a, b, preferred_element_type=F32)


def _ln(x):
    mu = jnp.mean(x, axis=-1, keepdims=True)
    d = x - mu
    var = jnp.mean(d * d, axis=-1, keepdims=True)
    return d * lax.rsqrt(var + LN_EPS)


def _gelu(x):
    return 0.5 * x * (1.0 + lax.erf(x * (1.0 / math.sqrt(2.0))))


def _cparams(*sem):
    return pltpu.CompilerParams(dimension_semantics=sem, vmem_limit_bytes=VMEM_LIMIT)


def _split_bf16(t):
    hi = t.astype(BF16)
    return hi, (t - hi.astype(F32)).astype(BF16)


def _mod_kernel(c_ref, w_ref, b_ref, o_ref):
    c = c_ref[...]
    s = (c * jax.nn.sigmoid(c)).astype(BF16)
    o_ref[...] = _dot(s, w_ref[...].astype(BF16)) + b_ref[...]


def _mod_call(cc, w_mod, b_mod):
    depth, d, nm = w_mod.shape
    tn = 1536
    return pl.pallas_call(
        _mod_kernel,
        grid=(depth, nm // tn),
        in_specs=[
            pl.BlockSpec((SUBLANES, d), lambda l, j: (0, 0)),
            pl.BlockSpec((None, d, tn), lambda l, j: (l, 0, j)),
            pl.BlockSpec((None, 1, tn), lambda l, j: (l, 0, j)),
        ],
        out_specs=pl.BlockSpec((None, SUBLANES, tn), lambda l, j: (l, 0, j)),
        out_shape=jax.ShapeDtypeStruct((depth, SUBLANES, nm), F32),
        compiler_params=_cparams("parallel", "parallel"),
        name="mod_vectors",
    )(cc, w_mod, b_mod.reshape(depth, 1, nm))


def _dot_3pass(a, b):
    a_hi, a_lo = _split_bf16(a)
    b_hi, b_lo = _split_bf16(b)
    return _dot(a_hi, b_hi) + (_dot(a_hi, b_lo) + _dot(a_lo, b_hi))


def _fold_kernel(cc_ref, sc_ref, wf_ref, o_ref):
    wf = wf_ref[...]
    n = wf.shape[0]
    o_ref[:, :n] = _dot_3pass(cc_ref[...], wf)
    o_ref[:, n:] = _dot_3pass(sc_ref[...], wf)


def _fold_call(cc_bd, sc_bd, wf_bd):
    n = wf_bd.shape[0]
    return pl.pallas_call(
        _fold_kernel,
        out_shape=jax.ShapeDtypeStruct((n, 2 * n), F32),
        name="fourier_weight_fold",
    )(cc_bd, sc_bd, wf_bd)


def _inproj_kernel(*refs, add_pos, d_main):
    if add_pos:
        x_ref, prow_ref, pcol_ref, mod_ref, w_ref, wcs_ref, xo_ref, pm_ref, pq_ref = refs
    else:
        x_ref, mod_ref, w_ref, wcs_ref, pm_ref, pq_ref = refs
    tile, d = x_ref.shape
    m = mod_ref[...]
    sub = min(tile, INPROJ_ROWS)

    def normed(k):
        rows = pl.ds(k * sub, sub)
        if add_pos:
            half = d // 2
            nrow = sub // GRID_W
            prow = prow_ref[pl.ds(k * nrow, nrow), :]
            xr = x_ref[rows, :half].reshape(nrow, GRID_W, half) + prow[:, None, :]
            xq = x_ref[rows, half:].reshape(nrow, GRID_W, half) + pcol_ref[...][None, :, :]
            x = jnp.concatenate([xr.reshape(sub, half), xq.reshape(sub, half)], axis=-1)
            xo_ref[rows, :] = x
        else:
            x = x_ref[rows, :]
        return (_ln(x) * (1.0 + m[1:2, :]) + m[0:1, :]).astype(BF16)

    nsub = tile // sub
    h = normed(0)
    for k in range(nsub):
        rows = pl.ds(k * sub, sub)
        p = _dot(h, w_ref[...])
        if k + 1 < nsub:
            h = normed(k + 1)
        pm_ref[rows, :] = p[:, :d_main].astype(pm_ref.dtype)
        pq_ref[rows, :] = _dot(p[:, d_main:].astype(BF16), wcs_ref[...]).astype(pq_ref.dtype)


def _inproj_call(x, pos, mod, w_in, layer, wcs, tile):
    b, l, d = x.shape
    d_in = w_in.shape[2]
    d_c = wcs.shape[0]
    d_main = d_in - d_c
    add_pos = pos is not None
    tok = lambda bi, i: (bi, i, 0)
    in_specs = [pl.BlockSpec((None, tile, d), tok)]
    args = [x]
    if add_pos:
        prow, pcol = pos
        in_specs.append(pl.BlockSpec((tile // GRID_W, d // 2), lambda bi, i: (i, 0)))
        in_specs.append(pl.BlockSpec((GRID_W, d // 2), lambda bi, i: (0, 0)))
        args += [prow, pcol]
    in_specs += [
        pl.BlockSpec((None, N_MOD, d), lambda bi, i: (bi, 0, 0)),
        pl.BlockSpec((None, d, d_in), lambda bi, i: (layer, 0, 0)),
        pl.BlockSpec((d_c, 2 * d_c), lambda bi, i: (0, 0)),
    ]
    args += [mod, w_in, wcs]
    out_specs = [pl.BlockSpec((None, tile, d_main), tok), pl.BlockSpec((None, tile, 2 * d_c), tok)]
    out_shape = [jax.ShapeDtypeStruct((b, l, d_main), BF16), jax.ShapeDtypeStruct((b, l, 2 * d_c), BF16)]
    if add_pos:
        out_specs.insert(0, pl.BlockSpec((None, tile, d), tok))
        out_shape.insert(0, jax.ShapeDtypeStruct((b, l, d), F32))
    return pl.pallas_call(
        functools.partial(_inproj_kernel, add_pos=add_pos, d_main=d_main),
        grid=(b, l // tile),
        in_specs=in_specs,
        out_specs=out_specs,
        out_shape=out_shape,
        compiler_params=_cparams("parallel", "parallel"),
        name="in_proj",
    )(*args)


def _scan_scratch(tile, d_a):
    G = tile // SUBLANES
    pitch = G + SUBLANES
    return [
        pltpu.VMEM((d_a // LANES, SUBLANES * pitch, LANES), F32),
        pltpu.VMEM((G + 3, SUBLANES, d_a), F32),
        pltpu.VMEM((G, SUBLANES, d_a), F32),
        pltpu.VMEM((G, SUBLANES, d_a), F32),
        pltpu.VMEM((G, SUBLANES, d_a), F32),
        pltpu.VMEM((G, SUBLANES, d_a), F32),
        pltpu.VMEM((SUBLANES, d_a), F32),
    ]


def _scan_tile(x_ref, prev_ref, next_ref, cw_ref, cb_ref, wg_ref, ba_ref, bx_ref, lam_ref,
               out_ref, slab_ref, xt_ref, gr_ref, gi_ref, hl_ref, al_ref, carry_ref,
               *, reverse, seq_first, seq_last):
    tile, d_a = x_ref.shape
    G = tile // SUBLANES
    nq = d_a // LANES
    pitch = slab_ref.shape[1] // SUBLANES
    sub = lax.broadcasted_iota(jnp.int32, (SUBLANES, d_a), 0)
    bc = lambda row: jnp.broadcast_to(row, (SUBLANES, d_a))

    xv = x_ref[...].astype(F32)
    for s in range(SUBLANES):
        for q in range(nq):
            slab_ref[q, s * pitch:s * pitch + G, :] = xv[s * G:(s + 1) * G, q * LANES:(q + 1) * LANES]

    for g in range(G):
        for q in range(nq):
            xt_ref[g + 2, :, q * LANES:(q + 1) * LANES] = slab_ref[q, pl.ds(g, SUBLANES, stride=pitch), :]
    prev = jnp.where(seq_first, 0.0, prev_ref[...].astype(F32))
    nxt = jnp.where(seq_last, 0.0, next_ref[...].astype(F32))
    hrows = prev_ref.shape[0]
    xt_ref[0] = jnp.where(sub == 0, bc(prev[hrows - 2:hrows - 1, :]), pltpu.roll(xt_ref[G], 1, 0))
    xt_ref[1] = jnp.where(sub == 0, bc(prev[hrows - 1:hrows, :]), pltpu.roll(xt_ref[G + 1], 1, 0))
    xt_ref[G + 2] = jnp.where(sub == SUBLANES - 1, bc(nxt[0:1, :]), pltpu.roll(xt_ref[2], SUBLANES - 1, 0))
    cw = cw_ref[...]
    xh = (cb_ref[...][None] + cw[0:1, :][None] * xt_ref[0:G] + cw[1:2, :][None] * xt_ref[1:G + 1]
          + cw[2:3, :][None] * xt_ref[2:G + 2] + cw[3:4, :][None] * xt_ref[3:G + 3])
    xt_ref[0:G] = xh
    xhb = xh.reshape(tile, d_a).astype(BF16)

    pair = 2 * HEAD_DIM
    for j in range(d_a // pair):
        g2 = _dot(xhb[:, j * pair:(j + 1) * pair], wg_ref[j])
        gr_ref[:, :, j * pair:(j + 1) * pair] = g2[:, :pair].reshape(G, SUBLANES, pair)
        gi_ref[:, :, j * pair:(j + 1) * pair] = g2[:, pair:].reshape(G, SUBLANES, pair)
    lam = lam_ref[...]
    neg = -lam
    softplus = jnp.maximum(neg, 0.0) + jnp.log1p(jnp.exp(-jnp.abs(neg)))
    k2 = bc((-0.5 * RG_C / math.log(2.0)) * softplus)
    hba = bc(ba_ref[...])
    hbx = bc(bx_ref[...])

    h = jnp.zeros((SUBLANES, d_a), F32)
    acum = h + 1.0
    for t in range(G):
        g = (G - 1 - t) if reverse else t
        t_r = jnp.tanh(gr_ref[g] + hba)
        t_i = jnp.tanh(gi_ref[g] + hbx)
        l2 = k2 + k2 * t_r
        a = jnp.exp2(l2)
        w = jnp.tanh(l2 * (-math.log(2.0))) * (1.0 + a * a)
        xg = xt_ref[g]
        root = jnp.where(w > 0.0, w * lax.rsqrt(w), 0.0)
        h = a * h + root * (xg * t_i + xg)
        acum = acum * a
        hl_ref[g] = h
        al_ref[g] = acum

    first = SUBLANES - 1 if reverse else 0
    shift = SUBLANES - 1 if reverse else 1
    c_in = carry_ref[...]
    c = c_in
    for _ in range(SUBLANES - 1):
        c = jnp.where(sub == first, c_in, pltpu.roll(h + acum * c, shift, 0))
    last = SUBLANES - 1 - first
    carry_ref[...] = bc((h + acum * c)[last:last + 1, :])

    for g in range(G):
        hfull = hl_ref[g] + al_ref[g] * c
        for q in range(nq):
            slab_ref[q, pl.ds(g, SUBLANES, stride=pitch), :] = hfull[:, q * LANES:(q + 1) * LANES]
    for s in range(SUBLANES):
        for q in range(nq):
            out_ref[s * G:(s + 1) * G, q * LANES:(q + 1) * LANES] = (
                slab_ref[q, s * pitch:s * pitch + G, :].astype(out_ref.dtype))


def _lru_kernel(x_ref, prev_ref, next_ref, cw_ref, cb_ref, wg_ref, ba_ref, bx_ref, lam_ref, h0_ref,
                h_ref, hfin_ref, *scratch, reverse):
    i = pl.program_id(1)
    n = pl.num_programs(1)
    li = (n - 1 - i) if reverse else i

    @pl.when(i == 0)
    def _():
        scratch[-1][...] = h0_ref[...]

    _scan_tile(x_ref, prev_ref, next_ref, cw_ref, cb_ref, wg_ref, ba_ref, bx_ref, lam_ref,
               h_ref, *scratch, reverse=reverse, seq_first=li == 0, seq_last=li == n - 1)
    hfin_ref[...] = scratch[-1][...]


def _lru_call(p_main, conv_w_half, conv_b_half, wg, ba_half, bx_half, lam, h0, *, reverse, tile):
    b, l, _ = p_main.shape
    d_a = conv_w_half.shape[-1]
    n = l // tile
    per = tile // HALO
    nblk = l // HALO
    pos = (lambda i: n - 1 - i) if reverse else (lambda i: i)
    vec = lambda shape: pl.BlockSpec(shape, lambda bi, i: (0,) * len(shape))
    return pl.pallas_call(
        functools.partial(_lru_kernel, reverse=reverse),
        grid=(b, n),
        in_specs=[
            pl.BlockSpec((None, tile, d_a), lambda bi, i: (bi, pos(i), 0)),
            pl.BlockSpec((None, HALO, d_a), lambda bi, i: (bi, jnp.maximum(pos(i) * per - 1, 0), 0)),
            pl.BlockSpec((None, HALO, d_a), lambda bi, i: (bi, jnp.minimum((pos(i) + 1) * per, nblk - 1), 0)),
            vec((4, d_a)), vec((1, d_a)), vec(wg.shape), vec((1, d_a)), vec((1, d_a)), vec((1, d_a)),
            pl.BlockSpec((None, SUBLANES, d_a), lambda bi, i: (bi, 0, 0)),
        ],
        out_specs=[
            pl.BlockSpec((None, tile, d_a), lambda bi, i: (bi, pos(i), 0)),
            pl.BlockSpec((None, SUBLANES, d_a), lambda bi, i: (bi, 0, 0)),
        ],
        out_shape=[jax.ShapeDtypeStruct((b, l, d_a), BF16), jax.ShapeDtypeStruct((b, SUBLANES, d_a), F32)],
        scratch_shapes=_scan_scratch(tile, d_a),
        compiler_params=_cparams("arbitrary", "arbitrary"),
        name="lru_scan_bwd" if reverse else "lru_scan_fwd",
    )(p_main, p_main, p_main, conv_w_half, conv_b_half, wg, ba_half, bx_half, lam, h0)


def _dft_direct_kernel(pq_ref, m_ref, y_ref):
    pq = pq_ref[...]
    dc = pq.shape[-1] // 2
    rhs = jnp.concatenate([pq[:, :dc], pq[:, dc:]], axis=0).astype(BF16)
    y_ref[...] = _dot(m_ref[...], rhs).astype(y_ref.dtype)


def _dft_direct_call(pq, m):
    b, l, dc2 = pq.shape
    return pl.pallas_call(
        _dft_direct_kernel,
        grid=(b,),
        in_specs=[pl.BlockSpec((None, l, dc2), lambda bi: (bi, 0, 0)),
                  pl.BlockSpec((l, 2 * l), lambda bi: (0, 0))],
        out_specs=pl.BlockSpec((None, l, dc2 // 2), lambda bi: (bi, 0, 0)),
        out_shape=jax.ShapeDtypeStruct((b, l, dc2 // 2), BF16),
        compiler_params=_cparams("parallel"),
        name="dft_direct",
    )(pq, m)


def _slab_rows(a, width):
    return pl.ds(a * DFT_PITCH, width)


def _dft_stage1_kernel(x_ref, m_ref, ct_ref, st_ref, br_ref, bi_ref, xs_ref, os_ref):
    n1, tl, w = x_ref.shape
    dc = w // 2
    for a in range(n1):
        xa = x_ref[a].astype(F32)
        for q in range(w // LANES):
            xs_ref[q, _slab_rows(a, tl), :] = xa[:, q * LANES:(q + 1) * LANES]
    m = m_ref[...]
    for t in range(tl):
        every = pl.ds(t, n1, stride=DFT_PITCH)
        x = jnp.concatenate([xs_ref[q, every, :] for q in range(w // LANES)], axis=-1).astype(BF16)
        cs = _dot(m, x)
        ar = cs[:n1, :dc] - cs[n1:, dc:]
        ai = cs[:n1, dc:] + cs[n1:, :dc]
        ct = jnp.concatenate([ct_ref[t]] * (dc // LANES), axis=-1)
        st = jnp.concatenate([st_ref[t]] * (dc // LANES), axis=-1)
        br = ct * ar - st * ai
        bi = ct * ai + st * ar
        for q in range(dc // LANES):
            os_ref[0, q, every, :] = br[:, q * LANES:(q + 1) * LANES]
            os_ref[1, q, every, :] = bi[:, q * LANES:(q + 1) * LANES]
    for a in range(n1):
        for part, o_ref in enumerate((br_ref, bi_ref)):
            o_ref[a] = jnp.concatenate([os_ref[part, q, _slab_rows(a, tl), :] for q in range(dc // LANES)],
                                       axis=-1).astype(o_ref.dtype)


def _dft_stage1_call(x4, m1, ct, st):
    b, n1, n2, w = x4.shape
    dc = w // 2
    tl = min(DFT_TILE, n2)
    out_blk = pl.BlockSpec((None, n1, tl, dc), lambda j, bi: (bi, 0, j, 0))
    tab_blk = pl.BlockSpec((tl, n1, LANES), lambda j, bi: (j, 0, 0))
    return pl.pallas_call(
        _dft_stage1_kernel,
        grid=(n2 // tl, b),
        in_specs=[
            pl.BlockSpec((None, n1, tl, w), lambda j, bi: (bi, 0, j, 0)),
            pl.BlockSpec((2 * n1, n1), lambda j, bi: (0, 0)),
            tab_blk, tab_blk,
        ],
        out_specs=[out_blk, out_blk],
        out_shape=[jax.ShapeDtypeStruct((b, n1, n2, dc), BF16)] * 2,
        scratch_shapes=[pltpu.VMEM((w // LANES, n1 * DFT_PITCH, LANES), F32),
                        pltpu.VMEM((2, dc // LANES, n1 * DFT_PITCH, LANES), F32)],
        compiler_params=_cparams("parallel", "parallel"),
        name="dft_stage1",
    )(x4, m1, ct, st)


def _dft_stage2_kernel(br_ref, bi_ref, m_ref, y_ref, ys_ref):
    tk, n2, dc = br_ref.shape
    m = m_ref[...]
    for a in range(tk):
        y = _dot(m, jnp.concatenate([br_ref[a], bi_ref[a]], axis=0).astype(BF16))
        for q in range(dc // LANES):
            ys_ref[q, pl.ds(a, n2, stride=DFT_PITCH), :] = y[:, q * LANES:(q + 1) * LANES]
    for k2 in range(n2):
        y_ref[k2] = jnp.concatenate([ys_ref[q, _slab_rows(k2, tk), :] for q in range(dc // LANES)],
                                    axis=-1).astype(y_ref.dtype)


def _dft_stage2_call(br, bi, m2):
    b, n1, n2, dc = br.shape
    tk = min(DFT_TILE, n1)
    blk = pl.BlockSpec((None, tk, n2, dc), lambda bi_, k: (bi_, k, 0, 0))
    return pl.pallas_call(
        _dft_stage2_kernel,
        grid=(b, n1 // tk),
        in_specs=[blk, blk, pl.BlockSpec((n2, 2 * n2), lambda bi_, k: (0, 0))],
        out_specs=pl.BlockSpec((None, n2, tk, dc), lambda bi_, k: (bi_, 0, k, 0)),
        out_shape=jax.ShapeDtypeStruct((b, n2, n1, dc), BF16),
        scratch_shapes=[pltpu.VMEM((dc // LANES, n2 * DFT_PITCH, LANES), F32)],
        compiler_params=_cparams("parallel", "parallel"),
        name="dft_stage2",
    )(br, bi, m2)


def _dft_tables(l, scale):
    n2 = DFT_N2
    n1 = l // n2

    def cs(num, den):
        ang = (2.0 * np.pi / den) * (num % den).astype(np.float64)
        return np.cos(ang), np.sin(ang)

    k1 = np.arange(n1)
    c1, s1 = cs(np.outer(k1, k1), n1)
    l2 = np.arange(n2)
    ct, st = cs(np.outer(k1, l2), l)
    c2, s2 = cs(np.outer(l2, l2), n2)
    return c1, s1, ct, st, c2 * scale, s2 * scale


def _fourier_latent(pq, dc):
    b, l, _ = pq.shape
    n2 = DFT_N2
    n1 = l // n2
    c1, s1, ct, st, c2, s2 = _dft_tables(l, 1.0 / math.sqrt(l * HEAD_DIM))
    m1 = jnp.asarray(np.concatenate([c1, s1], axis=0), F32).astype(BF16)
    m2 = jnp.asarray(np.concatenate([c2, -s2], axis=1), F32).astype(BF16)
    lanes = lambda t: jnp.asarray(np.broadcast_to(t.T[:, :, None], (n2, n1, LANES)), F32)
    br, bi = _dft_stage1_call(pq.reshape(b, n1, n2, 2 * dc), m1, lanes(ct), lanes(st))
    y = _dft_stage2_call(br, bi, m2)
    return y.reshape(b, l, dc)


def _fourier_direct(pq, dc):
    b, l, _ = pq.shape
    k = np.arange(l)
    ang = (2.0 * np.pi / l) * (np.outer(k, k) % l).astype(np.float64)
    scale = 1.0 / math.sqrt(l * HEAD_DIM)
    m = jnp.asarray(np.concatenate([np.cos(ang) * scale, -np.sin(ang) * scale], axis=1), F32).astype(BF16)
    return _dft_direct_call(pq, m)


MIX_STAGES = 5


def _mix_rows(rows, out_ref, hn_ref, gate_ref, uv_ref, hf_ref, hb_ref, yc_ref, x_ref, mod_ref, seg_ref, ws_ref,
              sgb_ref, gmix_ref, wout_ref, g_ref, b_ref, *, alpha, d_a, d_b):
    seg = seg_ref[...]
    inv = 1.0 / HEAD_DIM

    def seg_mean(hi, lo):
        return (_dot(hi, seg) + _dot(lo, seg)) * inv

    def rms(t, g):
        return t * lax.rsqrt(jnp.mean(t * t, axis=-1, keepdims=True) + LN_EPS) * g

    gm = gmix_ref[...]
    z = _gelu(uv_ref[rows, :].astype(F32))
    u = z[:, :d_b]
    v = z[:, d_b:]
    v_hi, v_lo = _split_bf16(v)
    yield
    dv = v - seg_mean(v_hi, v_lo)
    q_hi, q_lo = _split_bf16(dv * dv)
    ya = rms(_gelu(gate_ref[rows, :].astype(F32)) * (hf_ref[rows, :].astype(F32) + hb_ref[rows, :].astype(F32)),
             gm[:, :d_a])
    yield
    vn = dv * lax.rsqrt(seg_mean(q_hi, q_lo) + LN_EPS)
    lane = lax.broadcasted_iota(jnp.int32, (CHUNK, d_b), 1)
    heads = d_b // HEAD_DIM
    rhs = jnp.concatenate(
        [jnp.where((lane >= hh * HEAD_DIM) & (lane < (hh + 1) * HEAD_DIM), vn, 0.0) for hh in range(heads)],
        axis=0).astype(BF16)
    yc = rms(yc_ref[rows, :].astype(F32), gm[:, d_a + d_b:])
    yield
    yb = rms(u * (_dot(ws_ref[...], rhs) + sgb_ref[...]), gm[:, d_a:d_a + d_b])
    y = jnp.concatenate([ya, yb, yc], axis=-1).astype(BF16)
    yield
    o = _dot(y, wout_ref[...])
    m = mod_ref[...]
    x1 = _ln(alpha * x_ref[rows, :] + m[2:3, :] * o) * g_ref[...] + b_ref[...]
    out_ref[rows, :] = x1
    hn_ref[rows, :] = (_ln(x1) * (1.0 + m[4:5, :]) + m[3:4, :]).astype(hn_ref.dtype)
    yield


def _ffn_body(h_ref, x_ref, m, wup_ref, wdn_ref, g_ref, b_ref, between, *, alpha):
    h = h_ref[...]
    d_ff = wdn_ref.shape[0]
    for j in range(d_ff // FF_CHUNK):
        cols = pl.ds(j * FF_CHUNK, FF_CHUNK)
        g = _dot(h, wup_ref[:, cols])
        up = _dot(h, wup_ref[:, pl.ds(d_ff + j * FF_CHUNK, FF_CHUNK)])
        act = (g * jax.nn.sigmoid(g) * up).astype(BF16)
        part = _dot(act, wdn_ref[cols, :])
        acc = part if j == 0 else acc + part
        between(j)
    return _ln(alpha * x_ref[...] + m[5:6, :] * acc) * g_ref[...] + b_ref[...]


def _mixffn_kernel(gate_ref, uv_ref, hf_ref, hb_ref, yc_ref, x_ref, modm_ref, modf_ref, seg_ref, ws_ref, sgb_ref,
                   gmix_ref, wout_ref, g1_ref, b1_ref, wup_ref, wdn_ref, g2_ref, b2_ref,
                   o_ref, x1_ref, hn_ref, *, alpha, d_a, d_b, steps):
    step = pl.program_id(0)
    last = steps - 1
    nchunk = x_ref.shape[0] // CHUNK
    nff = wdn_ref.shape[0] // FF_CHUNK

    def mix_stages(slot):
        todo = []
        for k0 in range(0, nchunk, 2):
            gens = [_mix_rows(pl.ds(c * CHUNK, CHUNK), x1_ref.at[slot], hn_ref.at[slot], gate_ref, uv_ref, hf_ref,
                              hb_ref, yc_ref, x_ref, modm_ref, seg_ref, ws_ref, sgb_ref, gmix_ref, wout_ref,
                              g1_ref, b1_ref, alpha=alpha, d_a=d_a, d_b=d_b)
                    for c in range(k0, min(k0 + 2, nchunk))]
            todo += [g for _ in range(MIX_STAGES) for g in gens]
        return todo

    def ffn(slot, between):
        o_ref[...] = _ffn_body(hn_ref.at[slot], x1_ref.at[slot], modf_ref[...], wup_ref, wdn_ref, g2_ref, b2_ref,
                               between, alpha=alpha)

    @pl.when(step == 0)
    def _():
        for gen in mix_stages(0):
            next(gen)

    for parity in range(2):
        @pl.when((step > 0) & (step < last) & (step % 2 == parity))
        def _():
            todo = mix_stages(parity)
            per_gap = -(-len(todo) // (nff - 1))

            def between(j):
                count = len(todo) if j == nff - 1 else per_gap
                for _ in range(min(count, len(todo))):
                    next(todo.pop(0))

            ffn(1 - parity, between)

    @pl.when(step == last)
    def _():
        ffn((last - 1) % 2, lambda j: None)


def _mixffn_call(p_main, hf, hb, yc, x, mod, seg, ws_cat, sgb, g_mix, w_out, g1, b1, w_up, w_down, g2, b2,
                 *, layer, alpha, tile):
    b, l, d = x.shape
    d_a = hf.shape[-1]
    d_c = yc.shape[-1]
    d_b = d_c
    n = l // tile
    total = b * n

    def mixed(j):
        def index(s):
            t = jnp.minimum(s, total - 1)
            return (t // n, t % n, j)
        return index

    def fed(s):
        t = jnp.maximum(s - 1, 0)
        return (t // n, t % n, 0)

    vec = lambda shape: pl.BlockSpec(shape, lambda s: (0,) * len(shape))
    res = lambda shape: pl.BlockSpec((None,) + shape, lambda s: (layer, 0, 0), pipeline_mode=pl.Buffered(1))
    return pl.pallas_call(
        functools.partial(_mixffn_kernel, alpha=alpha, d_a=d_a, d_b=d_b, steps=total + 1),
        grid=(total + 1,),
        in_specs=[
            pl.BlockSpec((None, tile, d_a), mixed(1)),
            pl.BlockSpec((None, tile, 2 * d_b), mixed(2)),
            pl.BlockSpec((None, tile, d_a), mixed(0)),
            pl.BlockSpec((None, tile, d_a), mixed(0)),
            pl.BlockSpec((None, tile, d_c), mixed(0)),
            pl.BlockSpec((None, tile, d), mixed(0)),
            pl.BlockSpec((None, N_MOD, d), lambda s: (jnp.minimum(s, total - 1) // n, 0, 0)),
            pl.BlockSpec((None, N_MOD, d), lambda s: (jnp.maximum(s - 1, 0) // n, 0, 0)),
            vec(seg.shape), vec(ws_cat.shape), vec(sgb.shape), vec((1, d)),
            pl.BlockSpec((None,) + w_out.shape[1:], lambda s: (layer, 0, 0)),
            vec((1, d)), vec((1, d)),
            res(w_up.shape[1:]), res(w_down.shape[1:]), vec((1, d)), vec((1, d)),
        ],
        out_specs=pl.BlockSpec((None, tile, d), fed),
        out_shape=jax.ShapeDtypeStruct((b, l, d), F32),
        scratch_shapes=[pltpu.VMEM((2, tile, d), F32), pltpu.VMEM((2, tile, d), BF16)],
        compiler_params=_cparams("arbitrary"),
        name="mix_ffn",
    )(p_main, p_main, hf, hb, yc, x, mod, mod, seg, ws_cat, sgb, g_mix, w_out, g1, b1, w_up, w_down, g2, b2)


def _pos_tables(rows, dim):
    quarter = dim // 4
    freqs = POS_BASE ** (-jnp.arange(quarter, dtype=F32) / quarter)

    def enc(p):
        ang = p[:, None] * freqs[None, :]
        return jnp.concatenate([jnp.sin(ang), jnp.cos(ang)], -1)

    return enc(jnp.arange(rows, dtype=F32)), enc(jnp.arange(GRID_W, dtype=F32))


def _block_diag(blocks):
    n, r, c = blocks.shape
    eye = jnp.eye(n, dtype=blocks.dtype)
    return (eye[:, None, :, None] * blocks[:, :, None, :]).reshape(n * r, n * c)


def _gate_weights(wa, wx):
    heads = wa.shape[0]
    pa = jnp.stack([_block_diag(wa[2 * j:2 * j + 2]) for j in range(heads // 2)])
    px = jnp.stack([_block_diag(wx[2 * j:2 * j + 2]) for j in range(heads // 2)])
    return jnp.concatenate([pa, px], axis=-1).astype(BF16)


def kernel(x, c, ctx, c_ctx, w_mod, b_mod, w_in, conv_w, conv_b, lru_wa, lru_ba, lru_wx, lru_bx, lru_lam,
           sg_ws, sg_b, fourier_w, g_mix, w_out, ln1_g, ln1_b, w_up, w_down, ln2_g, ln2_b):
    bsz, seq, d = x.shape
    ctx_len = ctx.shape[1]
    depth = w_mod.shape[0]
    d_a = conv_w.shape[-1]
    d_c = fourier_w.shape[1] * fourier_w.shape[2]
    alpha = (2 * depth) ** 0.25
    lat_tile = min(512, seq)
    inproj_tile = min(2 * INPROJ_ROWS, seq)
    ctx_tile = min(256, ctx_len)

    cc = jnp.zeros((SUBLANES, d), F32).at[:bsz].set(c).at[bsz].set(c_ctx)
    mod = _mod_call(cc, w_mod, b_mod).reshape(depth, SUBLANES, N_MOD, d)

    pos = _pos_tables(seq // GRID_W, d)
    k = np.arange(HEAD_DIM)
    ang = (2.0 * np.pi / HEAD_DIM) * (np.outer(k, k) % HEAD_DIM).astype(np.float64)
    groups_c = d_c // HEAD_DIM
    cc_bd = jnp.asarray(np.kron(np.eye(groups_c), np.cos(ang)), F32)
    sc_bd = jnp.asarray(np.kron(np.eye(groups_c), np.sin(ang)), F32)
    seg = jnp.asarray(np.kron(np.eye(groups_c), np.ones((HEAD_DIM, HEAD_DIM))), BF16)
    zeros_state = jnp.zeros((bsz, SUBLANES, d_a), F32)
    w_in_b, w_out_b, w_up_b, w_dn_b = (w.astype(BF16) for w in (w_in, w_out, w_up, w_down))

    xl, xc = x, ctx
    for l in range(depth):
        last = l == depth - 1
        mod_l = mod[l, :bsz]
        mod_c = jnp.broadcast_to(mod[l, bsz][None], (bsz, N_MOD, d))
        wcs = _fold_call(cc_bd, sc_bd, _block_diag(fourier_w[l])).astype(BF16)
        cw, cb = 0.5 * conv_w[l], 0.5 * conv_b[l].reshape(1, d_a)
        lru = [(cw, cb, _gate_weights(lru_wa[l, dd], lru_wx[l, dd]), 0.5 * lru_ba[l, dd].reshape(1, d_a),
                0.5 * lru_bx[l, dd].reshape(1, d_a), lru_lam[l, dd].reshape(1, d_a)) for dd in range(2)]
        ws_cat = jnp.concatenate([sg_ws[l, hh] for hh in range(sg_ws.shape[1])], axis=1).astype(BF16)
        sgb = jnp.repeat(jnp.transpose(sg_b[l]), HEAD_DIM, axis=1)
        g1, b1 = ln1_g[l].reshape(1, d), ln1_b[l].reshape(1, d)
        g2, b2 = ln2_g[l].reshape(1, d), ln2_b[l].reshape(1, d)
        gm = g_mix[l].reshape(1, d)

        def mix_ffn(p_main, hf, hb, yc, xs, mods, tile):
            return _mixffn_call(p_main, hf, hb, yc, xs, mods, seg, ws_cat, sgb, gm, w_out_b, g1, b1,
                                w_up_b, w_dn_b, g2, b2, layer=l, alpha=alpha, tile=tile)

        pc_main, pc_pq = _inproj_call(xc, None, mod_c, w_in_b, l, wcs, ctx_tile)
        hf_c, sf_c = _lru_call(pc_main, *lru[0], zeros_state, reverse=False, tile=ctx_tile)
        hb_c, sb_c = _lru_call(pc_main, *lru[1], zeros_state, reverse=True, tile=ctx_tile)
        if not last:
            flat = lambda a: a.reshape(1, bsz * ctx_len, a.shape[-1])
            xc = mix_ffn(flat(pc_main), flat(hf_c), flat(hb_c), flat(_fourier_direct(pc_pq, d_c)), flat(xc),
                         mod_c[:1], math.gcd(bsz * ctx_len, lat_tile)).reshape(bsz, ctx_len, d)

        if l == 0:
            xl, pl_main, pl_pq = _inproj_call(xl, pos, mod_l, w_in_b, l, wcs, inproj_tile)
        else:
            pl_main, pl_pq = _inproj_call(xl, None, mod_l, w_in_b, l, wcs, min(2 * inproj_tile, seq))
        hf, _ = _lru_call(pl_main, *lru[0], sf_c, reverse=False, tile=lat_tile)
        hb, _ = _lru_call(pl_main, *lru[1], sb_c, reverse=True, tile=lat_tile)
        if seq > 2 * DFT_N2:
            yc = _fourier_latent(pl_pq, d_c)
        else:
            yc = _fourier_direct(pl_pq, d_c)
        xl = mix_ffn(pl_main, hf, hb, yc, xl, mod_l, lat_tile)
    return xl
```

```python
import functools
import math

import jax
import jax.numpy as jnp
import numpy as np
from jax import lax
from jax.experimental import pallas as pl
from jax.experimental.pallas import tpu as pltpu

HEAD_DIM = 64
CHUNK = 128
GRID_W = 64
RG_C = 8.0
LN_EPS = 1e-6
POS_BASE = 10000.0
N_MOD = 6
SUBLANES = 8
LANES = 128
HALO = 2 * SUBLANES
FF_CHUNK = 256
DFT_N2 = 128
DFT_TILE = 2 * HALO
DFT_PITCH = DFT_TILE + SUBLANES
INPROJ_ROWS = 512
VMEM_LIMIT = 56 * 1024 * 1024

F32 = jnp.float32
BF16 = jnp.bfloat16


def _dot(a, b):
    return jnp.dot(a, b, preferred_element_type=F32)


def _ln(x):
    mu = jnp.mean(x, axis=-1, keepdims=True)
    d = x - mu
    var = jnp.mean(d * d, axis=-1, keepdims=True)
    return d * lax.rsqrt(var + LN_EPS)


def _gelu(x):
    return 0.5 * x * (1.0 + lax.erf(x * (1.0 / math.sqrt(2.0))))


def _cparams(*sem):
    return pltpu.CompilerParams(dimension_semantics=sem, vmem_limit_bytes=VMEM_LIMIT)


def _split_bf16(t):
    hi = t.astype(BF16)
    return hi, (t - hi.astype(F32)).astype(BF16)


def _mod_kernel(c_ref, w_ref, b_ref, o_ref):
    c = c_ref[...]
    s = (c * jax.nn.sigmoid(c)).astype(BF16)
    o_ref[...] = _dot(s, w_ref[...].astype(BF16)) + b_ref[...]


def _mod_call(cc, w_mod, b_mod):
    depth, d, nm = w_mod.shape
    tn = 1536
    return pl.pallas_call(
        _mod_kernel,
        grid=(depth, nm // tn),
        in_specs=[
            pl.BlockSpec((SUBLANES, d), lambda l, j: (0, 0)),
            pl.BlockSpec((None, d, tn), lambda l, j: (l, 0, j)),
            pl.BlockSpec((None, 1, tn), lambda l, j: (l, 0, j)),
        ],
        out_specs=pl.BlockSpec((None, SUBLANES, tn), lambda l, j: (l, 0, j)),
        out_shape=jax.ShapeDtypeStruct((depth, SUBLANES, nm), F32),
        compiler_params=_cparams("parallel", "parallel"),
        name="mod_vectors",
    )(cc, w_mod, b_mod.reshape(depth, 1, nm))


def _dot_3pass(a, b):
    a_hi, a_lo = _split_bf16(a)
    b_hi, b_lo = _split_bf16(b)
    return _dot(a_hi, b_hi) + (_dot(a_hi, b_lo) + _dot(a_lo, b_hi))


def _fold_kernel(cc_ref, sc_ref, wf_ref, o_ref):
    wf = wf_ref[...]
    n = wf.shape[0]
    o_ref[:, :n] = _dot_3pass(cc_ref[...], wf)
    o_ref[:, n:] = _dot_3pass(sc_ref[...], wf)


def _fold_call(cc_bd, sc_bd, wf_bd):
    n = wf_bd.shape[0]
    return pl.pallas_call(
        _fold_kernel,
        out_shape=jax.ShapeDtypeStruct((n, 2 * n), F32),
        name="fourier_weight_fold",
    )(cc_bd, sc_bd, wf_bd)


def _inproj_kernel(*refs, add_pos, d_main):
    if add_pos:
        x_ref, prow_ref, pcol_ref, mod_ref, w_ref, wcs_ref, xo_ref, pm_ref, pq_ref = refs
    else:
        x_ref, mod_ref, w_ref, wcs_ref, pm_ref, pq_ref = refs
    tile, d = x_ref.shape
    m = mod_ref[...]
    sub = min(tile, INPROJ_ROWS)

    def normed(k):
        rows = pl.ds(k * sub, sub)
        if add_pos:
            half = d // 2
            nrow = sub // GRID_W
            prow = prow_ref[pl.ds(k * nrow, nrow), :]
            xr = x_ref[rows, :half].reshape(nrow, GRID_W, half) + prow[:, None, :]
            xq = x_ref[rows, half:].reshape(nrow, GRID_W, half) + pcol_ref[...][None, :, :]
            x = jnp.concatenate([xr.reshape(sub, half), xq.reshape(sub, half)], axis=-1)
            xo_ref[rows, :] = x
        else:
            x = x_ref[rows, :]
        return (_ln(x) * (1.0 + m[1:2, :]) + m[0:1, :]).astype(BF16)

    nsub = tile // sub
    h = normed(0)
    for k in range(nsub):
        rows = pl.ds(k * sub, sub)
        p = _dot(h, w_ref[...])
        if k + 1 < nsub:
            h = normed(k + 1)
        pm_ref[rows, :] = p[:, :d_main].astype(pm_ref.dtype)
        pq_ref[rows, :] = _dot(p[:, d_main:].astype(BF16), wcs_ref[...]).astype(pq_ref.dtype)


def _inproj_call(x, pos, mod, w_in, layer, wcs, tile):
    b, l, d = x.shape
    d_in = w_in.shape[2]
    d_c = wcs.shape[0]
    d_main = d_in - d_c
    add_pos = pos is not None
    tok = lambda bi, i: (bi, i, 0)
    in_specs = [pl.BlockSpec((None, tile, d), tok)]
    args = [x]
    if add_pos:
        prow, pcol = pos
        in_specs.append(pl.BlockSpec((tile // GRID_W, d // 2), lambda bi, i: (i, 0)))
        in_specs.append(pl.BlockSpec((GRID_W, d // 2), lambda bi, i: (0, 0)))
        args += [prow, pcol]
    in_specs += [
        pl.BlockSpec((None, N_MOD, d), lambda bi, i: (bi, 0, 0)),
        pl.BlockSpec((None, d, d_in), lambda bi, i: (layer, 0, 0)),
        pl.BlockSpec((d_c, 2 * d_c), lambda bi, i: (0, 0)),
    ]
    args += [mod, w_in, wcs]
    out_specs = [pl.BlockSpec((None, tile, d_main), tok), pl.BlockSpec((None, tile, 2 * d_c), tok)]
    out_shape = [jax.ShapeDtypeStruct((b, l, d_main), BF16), jax.ShapeDtypeStruct((b, l, 2 * d_c), BF16)]
    if add_pos:
        out_specs.insert(0, pl.BlockSpec((None, tile, d), tok))
        out_shape.insert(0, jax.ShapeDtypeStruct((b, l, d), F32))
    return pl.pallas_call(
        functools.partial(_inproj_kernel, add_pos=add_pos, d_main=d_main),
        grid=(b, l // tile),
        in_specs=in_specs,
        out_specs=out_specs,
        out_shape=out_shape,
        compiler_params=_cparams("parallel", "parallel"),
        name="in_proj",
    )(*args)


def _scan_scratch(tile, d_a):
    G = tile // SUBLANES
    pitch = G + SUBLANES
    return [
        pltpu.VMEM((d_a // LANES, SUBLANES * pitch, LANES), F32),
        pltpu.VMEM((G + 3, SUBLANES, d_a), F32),
        pltpu.VMEM((G, SUBLANES, d_a), F32),
        pltpu.VMEM((G, SUBLANES, d_a), F32),
        pltpu.VMEM((G, SUBLANES, d_a), F32),
        pltpu.VMEM((G, SUBLANES, d_a), F32),
        pltpu.VMEM((SUBLANES, d_a), F32),
    ]


def _scan_tile(x_ref, prev_ref, next_ref, cw_ref, cb_ref, wg_ref, ba_ref, bx_ref, lam_ref,
               out_ref, slab_ref, xt_ref, gr_ref, gi_ref, hl_ref, al_ref, carry_ref,
               *, reverse, seq_first, seq_last):
    tile, d_a = x_ref.shape
    G = tile // SUBLANES
    nq = d_a // LANES
    pitch = slab_ref.shape[1] // SUBLANES
    sub = lax.broadcasted_iota(jnp.int32, (SUBLANES, d_a), 0)
    bc = lambda row: jnp.broadcast_to(row, (SUBLANES, d_a))

    xv = x_ref[...].astype(F32)
    for s in range(SUBLANES):
        for q in range(nq):
            slab_ref[q, s * pitch:s * pitch + G, :] = xv[s * G:(s + 1) * G, q * LANES:(q + 1) * LANES]

    for g in range(G):
        for q in range(nq):
            xt_ref[g + 2, :, q * LANES:(q + 1) * LANES] = slab_ref[q, pl.ds(g, SUBLANES, stride=pitch), :]
    prev = jnp.where(seq_first, 0.0, prev_ref[...].astype(F32))
    nxt = jnp.where(seq_last, 0.0, next_ref[...].astype(F32))
    hrows = prev_ref.shape[0]
    xt_ref[0] = jnp.where(sub == 0, bc(prev[hrows - 2:hrows - 1, :]), pltpu.roll(xt_ref[G], 1, 0))
    xt_ref[1] = jnp.where(sub == 0, bc(prev[hrows - 1:hrows, :]), pltpu.roll(xt_ref[G + 1], 1, 0))
    xt_ref[G + 2] = jnp.where(sub == SUBLANES - 1, bc(nxt[0:1, :]), pltpu.roll(xt_ref[2], SUBLANES - 1, 0))
    cw = cw_ref[...]
    xh = (cb_ref[...][None] + cw[0:1, :][None] * xt_ref[0:G] + cw[1:2, :][None] * xt_ref[1:G + 1]
          + cw[2:3, :][None] * xt_ref[2:G + 2] + cw[3:4, :][None] * xt_ref[3:G + 3])
    xt_ref[0:G] = xh
    xhb = xh.reshape(tile, d_a).astype(BF16)

    pair = 2 * HEAD_DIM
    for j in range(d_a // pair):
        g2 = _dot(xhb[:, j * pair:(j + 1) * pair], wg_ref[j])
        gr_ref[:, :, j * pair:(j + 1) * pair] = g2[:, :pair].reshape(G, SUBLANES, pair)
        gi_ref[:, :, j * pair:(j + 1) * pair] = g2[:, pair:].reshape(G, SUBLANES, pair)
    lam = lam_ref[...]
    neg = -lam
    softplus = jnp.maximum(neg, 0.0) + jnp.log1p(jnp.exp(-jnp.abs(neg)))
    k2 = bc((-0.5 * RG_C / math.log(2.0)) * softplus)
    hba = bc(ba_ref[...])
    hbx = bc(bx_ref[...])

    h = jnp.zeros((SUBLANES, d_a), F32)
    acum = h + 1.0
    for t in range(G):
        g = (G - 1 - t) if reverse else t
        t_r = jnp.tanh(gr_ref[g] + hba)
        t_i = jnp.tanh(gi_ref[g] + hbx)
        l2 = k2 + k2 * t_r
        a = jnp.exp2(l2)
        w = jnp.tanh(l2 * (-math.log(2.0))) * (1.0 + a * a)
        xg = xt_ref[g]
        root = jnp.where(w > 0.0, w * lax.rsqrt(w), 0.0)
        h = a * h + root * (xg * t_i + xg)
        acum = acum * a
        hl_ref[g] = h
        al_ref[g] = acum

    first = SUBLANES - 1 if reverse else 0
    shift = SUBLANES - 1 if reverse else 1
    c_in = carry_ref[...]
    c = c_in
    for _ in range(SUBLANES - 1):
        c = jnp.where(sub == first, c_in, pltpu.roll(h + acum * c, shift, 0))
    last = SUBLANES - 1 - first
    carry_ref[...] = bc((h + acum * c)[last:last + 1, :])

    for g in range(G):
        hfull = hl_ref[g] + al_ref[g] * c
        for q in range(nq):
            slab_ref[q, pl.ds(g, SUBLANES, stride=pitch), :] = hfull[:, q * LANES:(q + 1) * LANES]
    for s in range(SUBLANES):
        for q in range(nq):
            out_ref[s * G:(s + 1) * G, q * LANES:(q + 1) * LANES] = (
                slab_ref[q, s * pitch:s * pitch + G, :].astype(out_ref.dtype))


def _lru_kernel(x_ref, prev_ref, next_ref, cw_ref, cb_ref, wg_ref, ba_ref, bx_ref, lam_ref, h0_ref,
                h_ref, hfin_ref, *scratch, reverse):
    i = pl.program_id(1)
    n = pl.num_programs(1)
    li = (n - 1 - i) if reverse else i

    @pl.when(i == 0)
    def _():
        scratch[-1][...] = h0_ref[...]

    _scan_tile(x_ref, prev_ref, next_ref, cw_ref, cb_ref, wg_ref, ba_ref, bx_ref, lam_ref,
               h_ref, *scratch, reverse=reverse, seq_first=li == 0, seq_last=li == n - 1)
    hfin_ref[...] = scratch[-1][...]


def _lru_call(p_main, conv_w_half, conv_b_half, wg, ba_half, bx_half, lam, h0, *, reverse, tile):
    b, l, _ = p_main.shape
    d_a = conv_w_half.shape[-1]
    n = l // tile
    per = tile // HALO
    nblk = l // HALO
    pos = (lambda i: n - 1 - i) if reverse else (lambda i: i)
    vec = lambda shape: pl.BlockSpec(shape, lambda bi, i: (0,) * len(shape))
    return pl.pallas_call(
        functools.partial(_lru_kernel, reverse=reverse),
        grid=(b, n),
        in_specs=[
            pl.BlockSpec((None, tile, d_a), lambda bi, i: (bi, pos(i), 0)),
            pl.BlockSpec((None, HALO, d_a), lambda bi, i: (bi, jnp.maximum(pos(i) * per - 1, 0), 0)),
            pl.BlockSpec((None, HALO, d_a), lambda bi, i: (bi, jnp.minimum((pos(i) + 1) * per, nblk - 1), 0)),
            vec((4, d_a)), vec((1, d_a)), vec(wg.shape), vec((1, d_a)), vec((1, d_a)), vec((1, d_a)),
            pl.BlockSpec((None, SUBLANES, d_a), lambda bi, i: (bi, 0, 0)),
        ],
        out_specs=[
            pl.BlockSpec((None, tile, d_a), lambda bi, i: (bi, pos(i), 0)),
            pl.BlockSpec((None, SUBLANES, d_a), lambda bi, i: (bi, 0, 0)),
        ],
        out_shape=[jax.ShapeDtypeStruct((b, l, d_a), BF16), jax.ShapeDtypeStruct((b, SUBLANES, d_a), F32)],
        scratch_shapes=_scan_scratch(tile, d_a),
        compiler_params=_cparams("arbitrary", "arbitrary"),
        name="lru_scan_bwd" if reverse else "lru_scan_fwd",
    )(p_main, p_main, p_main, conv_w_half, conv_b_half, wg, ba_half, bx_half, lam, h0)


def _dft_direct_kernel(pq_ref, m_ref, y_ref):
    pq = pq_ref[...]
    dc = pq.shape[-1] // 2
    rhs = jnp.concatenate([pq[:, :dc], pq[:, dc:]], axis=0).astype(BF16)
    y_ref[...] = _dot(m_ref[...], rhs).astype(y_ref.dtype)


def _dft_direct_call(pq, m):
    b, l, dc2 = pq.shape
    return pl.pallas_call(
        _dft_direct_kernel,
        grid=(b,),
        in_specs=[pl.BlockSpec((None, l, dc2), lambda bi: (bi, 0, 0)),
                  pl.BlockSpec((l, 2 * l), lambda bi: (0, 0))],
        out_specs=pl.BlockSpec((None, l, dc2 // 2), lambda bi: (bi, 0, 0)),
        out_shape=jax.ShapeDtypeStruct((b, l, dc2 // 2), BF16),
        compiler_params=_cparams("parallel"),
        name="dft_direct",
    )(pq, m)


def _slab_rows(a, width):
    return pl.ds(a * DFT_PITCH, width)


def _dft_stage1_kernel(x_ref, m_ref, ct_ref, st_ref, br_ref, bi_ref, xs_ref, os_ref):
    n1, tl, w = x_ref.shape
    dc = w // 2
    for a in range(n1):
        xa = x_ref[a].astype(F32)
        for q in range(w // LANES):
            xs_ref[q, _slab_rows(a, tl), :] = xa[:, q * LANES:(q + 1) * LANES]
    m = m_ref[...]
    for t in range(tl):
        every = pl.ds(t, n1, stride=DFT_PITCH)
        x = jnp.concatenate([xs_ref[q, every, :] for q in range(w // LANES)], axis=-1).astype(BF16)
        cs = _dot(m, x)
        ar = cs[:n1, :dc] - cs[n1:, dc:]
        ai = cs[:n1, dc:] + cs[n1:, :dc]
        ct = jnp.concatenate([ct_ref[t]] * (dc // LANES), axis=-1)
        st = jnp.concatenate([st_ref[t]] * (dc // LANES), axis=-1)
        br = ct * ar - st * ai
        bi = ct * ai + st * ar
        for q in range(dc // LANES):
            os_ref[0, q, every, :] = br[:, q * LANES:(q + 1) * LANES]
            os_ref[1, q, every, :] = bi[:, q * LANES:(q + 1) * LANES]
    for a in range(n1):
        for part, o_ref in enumerate((br_ref, bi_ref)):
            o_ref[a] = jnp.concatenate([os_ref[part, q, _slab_rows(a, tl), :] for q in range(dc // LANES)],
                                       axis=-1).astype(o_ref.dtype)


def _dft_stage1_call(x4, m1, ct, st):
    b, n1, n2, w = x4.shape
    dc = w // 2
    tl = min(DFT_TILE, n2)
    out_blk = pl.BlockSpec((None, n1, tl, dc), lambda j, bi: (bi, 0, j, 0))
    tab_blk = pl.BlockSpec((tl, n1, LANES), lambda j, bi: (j, 0, 0))
    return pl.pallas_call(
        _dft_stage1_kernel,
        grid=(n2 // tl, b),
        in_specs=[
            pl.BlockSpec((None, n1, tl, w), lambda j, bi: (bi, 0, j, 0)),
            pl.BlockSpec((2 * n1, n1), lambda j, bi: (0, 0)),
            tab_blk, tab_blk,
        ],
        out_specs=[out_blk, out_blk],
        out_shape=[jax.ShapeDtypeStruct((b, n1, n2, dc), BF16)] * 2,
        scratch_shapes=[pltpu.VMEM((w // LANES, n1 * DFT_PITCH, LANES), F32),
                        pltpu.VMEM((2, dc // LANES, n1 * DFT_PITCH, LANES), F32)],
        compiler_params=_cparams("parallel", "parallel"),
        name="dft_stage1",
    )(x4, m1, ct, st)


def _dft_stage2_kernel(br_ref, bi_ref, m_ref, y_ref, ys_ref):
    tk, n2, dc = br_ref.shape
    m = m_ref[...]
    for a in range(tk):
        y = _dot(m, jnp.concatenate([br_ref[a], bi_ref[a]], axis=0).astype(BF16))
        for q in range(dc // LANES):
            ys_ref[q, pl.ds(a, n2, stride=DFT_PITCH), :] = y[:, q * LANES:(q + 1) * LANES]
    for k2 in range(n2):
        y_ref[k2] = jnp.concatenate([ys_ref[q, _slab_rows(k2, tk), :] for q in range(dc // LANES)],
                                    axis=-1).astype(y_ref.dtype)


def _dft_stage2_call(br, bi, m2):
    b, n1, n2, dc = br.shape
    tk = min(DFT_TILE, n1)
    blk = pl.BlockSpec((None, tk, n2, dc), lambda bi_, k: (bi_, k, 0, 0))
    return pl.pallas_call(
        _dft_stage2_kernel,
        grid=(b, n1 // tk),
        in_specs=[blk, blk, pl.BlockSpec((n2, 2 * n2), lambda bi_, k: (0, 0))],
        out_specs=pl.BlockSpec((None, n2, tk, dc), lambda bi_, k: (bi_, 0, k, 0)),
        out_shape=jax.ShapeDtypeStruct((b, n2, n1, dc), BF16),
        scratch_shapes=[pltpu.VMEM((dc // LANES, n2 * DFT_PITCH, LANES), F32)],
        compiler_params=_cparams("parallel", "parallel"),
        name="dft_stage2",
    )(br, bi, m2)


def _dft_tables(l, scale):
    n2 = DFT_N2
    n1 = l // n2

    def cs(num, den):
        ang = (2.0 * np.pi / den) * (num % den).astype(np.float64)
        return np.cos(ang), np.sin(ang)

    k1 = np.arange(n1)
    c1, s1 = cs(np.outer(k1, k1), n1)
    l2 = np.arange(n2)
    ct, st = cs(np.outer(k1, l2), l)
    c2, s2 = cs(np.outer(l2, l2), n2)
    return c1, s1, ct, st, c2 * scale, s2 * scale


def _fourier_latent(pq, dc):
    b, l, _ = pq.shape
    n2 = DFT_N2
    n1 = l // n2
    c1, s1, ct, st, c2, s2 = _dft_tables(l, 1.0 / math.sqrt(l * HEAD_DIM))
    m1 = jnp.asarray(np.concatenate([c1, s1], axis=0), F32).astype(BF16)
    m2 = jnp.asarray(np.concatenate([c2, -s2], axis=1), F32).astype(BF16)
    lanes = lambda t: jnp.asarray(np.broadcast_to(t.T[:, :, None], (n2, n1, LANES)), F32)
    br, bi = _dft_stage1_call(pq.reshape(b, n1, n2, 2 * dc), m1, lanes(ct), lanes(st))
    y = _dft_stage2_call(br, bi, m2)
    return y.reshape(b, l, dc)


def _fourier_direct(pq, dc):
    b, l, _ = pq.shape
    k = np.arange(l)
    ang = (2.0 * np.pi / l) * (np.outer(k, k) % l).astype(np.float64)
    scale = 1.0 / math.sqrt(l * HEAD_DIM)
    m = jnp.asarray(np.concatenate([np.cos(ang) * scale, -np.sin(ang) * scale], axis=1), F32).astype(BF16)
    return _dft_direct_call(pq, m)


MIX_STAGES = 5


def _mix_rows(rows, out_ref, gate_ref, uv_ref, hf_ref, hb_ref, yc_ref, x_ref, mod_ref, seg_ref, ws_ref, sgb_ref,
              gmix_ref, wout_ref, g_ref, b_ref, *, alpha, d_a, d_b):
    seg = seg_ref[...]
    inv = 1.0 / HEAD_DIM

    def seg_mean(hi, lo):
        return (_dot(hi, seg) + _dot(lo, seg)) * inv

    def rms(t, g):
        return t * lax.rsqrt(jnp.mean(t * t, axis=-1, keepdims=True) + LN_EPS) * g

    gm = gmix_ref[...]
    z = _gelu(uv_ref[rows, :].astype(F32))
    u = z[:, :d_b]
    v = z[:, d_b:]
    v_hi, v_lo = _split_bf16(v)
    yield
    dv = v - seg_mean(v_hi, v_lo)
    q_hi, q_lo = _split_bf16(dv * dv)
    ya = rms(_gelu(gate_ref[rows, :].astype(F32)) * (hf_ref[rows, :].astype(F32) + hb_ref[rows, :].astype(F32)),
             gm[:, :d_a])
    yield
    vn = dv * lax.rsqrt(seg_mean(q_hi, q_lo) + LN_EPS)
    lane = lax.broadcasted_iota(jnp.int32, (CHUNK, d_b), 1)
    heads = d_b // HEAD_DIM
    rhs = jnp.concatenate(
        [jnp.where((lane >= hh * HEAD_DIM) & (lane < (hh + 1) * HEAD_DIM), vn, 0.0) for hh in range(heads)],
        axis=0).astype(BF16)
    yc = rms(yc_ref[rows, :].astype(F32), gm[:, d_a + d_b:])
    yield
    yb = rms(u * (_dot(ws_ref[...], rhs) + sgb_ref[...]), gm[:, d_a:d_a + d_b])
    y = jnp.concatenate([ya, yb, yc], axis=-1).astype(BF16)
    yield
    o = _dot(y, wout_ref[...])
    m = mod_ref[...]
    out_ref[rows, :] = _ln(alpha * x_ref[rows, :] + m[2:3, :] * o) * g_ref[...] + b_ref[...]
    yield


def _ffn_body(x, m, wup_ref, wdn_ref, g_ref, b_ref, res_ref, between, *, alpha):
    res_ref[...] = x
    h = (_ln(x) * (1.0 + m[4:5, :]) + m[3:4, :]).astype(BF16)
    d_ff = wdn_ref.shape[0]
    for j in range(d_ff // FF_CHUNK):
        cols = pl.ds(j * FF_CHUNK, FF_CHUNK)
        g = _dot(h, wup_ref[:, cols])
        up = _dot(h, wup_ref[:, pl.ds(d_ff + j * FF_CHUNK, FF_CHUNK)])
        act = (g * jax.nn.sigmoid(g) * up).astype(BF16)
        part = _dot(act, wdn_ref[cols, :])
        acc = part if j == 0 else acc + part
        between(j)
    return _ln(alpha * res_ref[...] + m[5:6, :] * acc) * g_ref[...] + b_ref[...]


def _mixffn_kernel(gate_ref, uv_ref, hf_ref, hb_ref, yc_ref, x_ref, modm_ref, modf_ref, seg_ref, ws_ref, sgb_ref,
                   gmix_ref, wout_ref, g1_ref, b1_ref, wup_ref, wdn_ref, g2_ref, b2_ref,
                   o_ref, x1_ref, res_ref, *, alpha, d_a, d_b):
    step = pl.program_id(0)
    nchunk = x_ref.shape[0] // CHUNK
    nff = wdn_ref.shape[0] // FF_CHUNK

    def mix_stages():
        todo = []
        for k0 in range(0, nchunk, 2):
            gens = [_mix_rows(pl.ds(c * CHUNK, CHUNK), x1_ref, gate_ref, uv_ref, hf_ref, hb_ref, yc_ref, x_ref,
                              modm_ref, seg_ref, ws_ref, sgb_ref, gmix_ref, wout_ref, g1_ref, b1_ref,
                              alpha=alpha, d_a=d_a, d_b=d_b) for c in range(k0, min(k0 + 2, nchunk))]
            todo += [g for _ in range(MIX_STAGES) for g in gens]
        return todo

    @pl.when(step == 0)
    def _():
        for gen in mix_stages():
            next(gen)

    @pl.when(step > 0)
    def _():
        todo = mix_stages()
        per_gap = -(-len(todo) // (nff - 1))

        def between(j):
            count = len(todo) if j == nff - 1 else per_gap
            for _ in range(min(count, len(todo))):
                next(todo.pop(0))

        o_ref[...] = _ffn_body(x1_ref[...], modf_ref[...], wup_ref, wdn_ref, g2_ref, b2_ref, res_ref, between,
                               alpha=alpha)


def _mixffn_call(p_main, hf, hb, yc, x, mod, seg, ws_cat, sgb, g_mix, w_out, g1, b1, w_up, w_down, g2, b2,
                 *, layer, alpha, tile):
    b, l, d = x.shape
    d_a = hf.shape[-1]
    d_c = yc.shape[-1]
    d_b = d_c
    n = l // tile
    total = b * n

    def mixed(j):
        def index(s):
            t = jnp.minimum(s, total - 1)
            return (t // n, t % n, j)
        return index

    def fed(s):
        t = jnp.maximum(s - 1, 0)
        return (t // n, t % n, 0)

    vec = lambda shape: pl.BlockSpec(shape, lambda s: (0,) * len(shape))
    res = lambda shape: pl.BlockSpec((None,) + shape, lambda s: (layer, 0, 0), pipeline_mode=pl.Buffered(1))
    return pl.pallas_call(
        functools.partial(_mixffn_kernel, alpha=alpha, d_a=d_a, d_b=d_b),
        grid=(total + 1,),
        in_specs=[
            pl.BlockSpec((None, tile, d_a), mixed(1)),
            pl.BlockSpec((None, tile, 2 * d_b), mixed(2)),
            pl.BlockSpec((None, tile, d_a), mixed(0)),
            pl.BlockSpec((None, tile, d_a), mixed(0)),
            pl.BlockSpec((None, tile, d_c), mixed(0)),
            pl.BlockSpec((None, tile, d), mixed(0)),
            pl.BlockSpec((None, N_MOD, d), lambda s: (jnp.minimum(s, total - 1) // n, 0, 0)),
            pl.BlockSpec((None, N_MOD, d), lambda s: (jnp.maximum(s - 1, 0) // n, 0, 0)),
            vec(seg.shape), vec(ws_cat.shape), vec(sgb.shape), vec((1, d)),
            pl.BlockSpec((None,) + w_out.shape[1:], lambda s: (layer, 0, 0)),
            vec((1, d)), vec((1, d)),
            res(w_up.shape[1:]), res(w_down.shape[1:]), vec((1, d)), vec((1, d)),
        ],
        out_specs=pl.BlockSpec((None, tile, d), fed),
        out_shape=jax.ShapeDtypeStruct((b, l, d), F32),
        scratch_shapes=[pltpu.VMEM((tile, d), F32)] * 2,
        compiler_params=_cparams("arbitrary"),
        name="mix_ffn",
    )(p_main, p_main, hf, hb, yc, x, mod, mod, seg, ws_cat, sgb, g_mix, w_out, g1, b1, w_up, w_down, g2, b2)


def _pos_tables(rows, dim):
    quarter = dim // 4
    freqs = POS_BASE ** (-jnp.arange(quarter, dtype=F32) / quarter)

    def enc(p):
        ang = p[:, None] * freqs[None, :]
        return jnp.concatenate([jnp.sin(ang), jnp.cos(ang)], -1)

    return enc(jnp.arange(rows, dtype=F32)), enc(jnp.arange(GRID_W, dtype=F32))


def _block_diag(blocks):
    n, r, c = blocks.shape
    eye = jnp.eye(n, dtype=blocks.dtype)
    return (eye[:, None, :, None] * blocks[:, :, None, :]).reshape(n * r, n * c)


def _gate_weights(wa, wx):
    heads = wa.shape[0]
    pa = jnp.stack([_block_diag(wa[2 * j:2 * j + 2]) for j in range(heads // 2)])
    px = jnp.stack([_block_diag(wx[2 * j:2 * j + 2]) for j in range(heads // 2)])
    return jnp.concatenate([pa, px], axis=-1).astype(BF16)


def kernel(x, c, ctx, c_ctx, w_mod, b_mod, w_in, conv_w, conv_b, lru_wa, lru_ba, lru_wx, lru_bx, lru_lam,
           sg_ws, sg_b, fourier_w, g_mix, w_out, ln1_g, ln1_b, w_up, w_down, ln2_g, ln2_b):
    bsz, seq, d = x.shape
    ctx_len = ctx.shape[1]
    depth = w_mod.shape[0]
    d_a = conv_w.shape[-1]
    d_c = fourier_w.shape[1] * fourier_w.shape[2]
    alpha = (2 * depth) ** 0.25
    lat_tile = min(512, seq)
    inproj_tile = min(2 * INPROJ_ROWS, seq)
    ctx_tile = min(256, ctx_len)

    cc = jnp.zeros((SUBLANES, d), F32).at[:bsz].set(c).at[bsz].set(c_ctx)
    mod = _mod_call(cc, w_mod, b_mod).reshape(depth, SUBLANES, N_MOD, d)

    pos = _pos_tables(seq // GRID_W, d)
    k = np.arange(HEAD_DIM)
    ang = (2.0 * np.pi / HEAD_DIM) * (np.outer(k, k) % HEAD_DIM).astype(np.float64)
    groups_c = d_c // HEAD_DIM
    cc_bd = jnp.asarray(np.kron(np.eye(groups_c), np.cos(ang)), F32)
    sc_bd = jnp.asarray(np.kron(np.eye(groups_c), np.sin(ang)), F32)
    seg = jnp.asarray(np.kron(np.eye(groups_c), np.ones((HEAD_DIM, HEAD_DIM))), BF16)
    zeros_state = jnp.zeros((bsz, SUBLANES, d_a), F32)
    w_in_b, w_out_b, w_up_b, w_dn_b = (w.astype(BF16) for w in (w_in, w_out, w_up, w_down))

    xl, xc = x, ctx
    for l in range(depth):
        last = l == depth - 1
        mod_l = mod[l, :bsz]
        mod_c = jnp.broadcast_to(mod[l, bsz][None], (bsz, N_MOD, d))
        wcs = _fold_call(cc_bd, sc_bd, _block_diag(fourier_w[l])).astype(BF16)
        cw, cb = 0.5 * conv_w[l], 0.5 * conv_b[l].reshape(1, d_a)
        lru = [(cw, cb, _gate_weights(lru_wa[l, dd], lru_wx[l, dd]), 0.5 * lru_ba[l, dd].reshape(1, d_a),
                0.5 * lru_bx[l, dd].reshape(1, d_a), lru_lam[l, dd].reshape(1, d_a)) for dd in range(2)]
        ws_cat = jnp.concatenate([sg_ws[l, hh] for hh in range(sg_ws.shape[1])], axis=1).astype(BF16)
        sgb = jnp.repeat(jnp.transpose(sg_b[l]), HEAD_DIM, axis=1)
        g1, b1 = ln1_g[l].reshape(1, d), ln1_b[l].reshape(1, d)
        g2, b2 = ln2_g[l].reshape(1, d), ln2_b[l].reshape(1, d)
        gm = g_mix[l].reshape(1, d)

        def mix_ffn(p_main, hf, hb, yc, xs, mods, tile):
            return _mixffn_call(p_main, hf, hb, yc, xs, mods, seg, ws_cat, sgb, gm, w_out_b, g1, b1,
                                w_up_b, w_dn_b, g2, b2, layer=l, alpha=alpha, tile=tile)

        pc_main, pc_pq = _inproj_call(xc, None, mod_c, w_in_b, l, wcs, ctx_tile)
        hf_c, sf_c = _lru_call(pc_main, *lru[0], zeros_state, reverse=False, tile=ctx_tile)
        hb_c, sb_c = _lru_call(pc_main, *lru[1], zeros_state, reverse=True, tile=ctx_tile)
        if not last:
            flat = lambda a: a.reshape(1, bsz * ctx_len, a.shape[-1])
            xc = mix_ffn(flat(pc_main), flat(hf_c), flat(hb_c), flat(_fourier_direct(pc_pq, d_c)), flat(xc),
                         mod_c[:1], math.gcd(bsz * ctx_len, lat_tile)).reshape(bsz, ctx_len, d)

        if l == 0:
            xl, pl_main, pl_pq = _inproj_call(xl, pos, mod_l, w_in_b, l, wcs, inproj_tile)
        else:
            pl_main, pl_pq = _inproj_call(xl, None, mod_l, w_in_b, l, wcs, min(2 * inproj_tile, seq))
        hf, _ = _lru_call(pl_main, *lru[0], sf_c, reverse=False, tile=lat_tile)
        hb, _ = _lru_call(pl_main, *lru[1], sb_c, reverse=True, tile=lat_tile)
        if seq > 2 * DFT_N2:
            yc = _fourier_latent(pl_pq, d_c)
        else:
            yc = _fourier_direct(pl_pq, d_c)
        xl = mix_ffn(pl_main, hf, hb, yc, xl, mod_l, lat_tile)
    return xl
```

```python
import functools
import math

import jax
import jax.numpy as jnp
import numpy as np
from jax import lax
from jax.experimental import pallas as pl
from jax.experimental.pallas import tpu as pltpu

HEAD_DIM = 64
CHUNK = 128
GRID_W = 64
RG_C = 8.0
LN_EPS = 1e-6
POS_BASE = 10000.0
N_MOD = 6
SUBLANES = 8
LANES = 128
HALO = 2 * SUBLANES
FF_CHUNK = 256
DFT_N2 = 128
DFT_TILE = 2 * HALO
DFT_PITCH = DFT_TILE + SUBLANES
INPROJ_ROWS = 512
VMEM_LIMIT = 56 * 1024 * 1024

F32 = jnp.float32
BF16 = jnp.bfloat16


def _dot(a, b):
    return jnp.dot(a, b, preferred_element_type=F32)


def _ln(x):
    mu = jnp.mean(x, axis=-1, keepdims=True)
    d = x - mu
    var = jnp.mean(d * d, axis=-1, keepdims=True)
    return d * lax.rsqrt(var + LN_EPS)


def _gelu(x):
    return 0.5 * x * (1.0 + lax.erf(x * (1.0 / math.sqrt(2.0))))


def _cparams(*sem):
    return pltpu.CompilerParams(dimension_semantics=sem, vmem_limit_bytes=VMEM_LIMIT)


def _split_bf16(t):
    hi = t.astype(BF16)
    return hi, (t - hi.astype(F32)).astype(BF16)


def _mod_kernel(c_ref, w_ref, b_ref, o_ref):
    c = c_ref[...]
    s = (c * jax.nn.sigmoid(c)).astype(BF16)
    o_ref[...] = _dot(s, w_ref[...].astype(BF16)) + b_ref[...]


def _mod_call(cc, w_mod, b_mod):
    depth, d, nm = w_mod.shape
    tn = 1536
    return pl.pallas_call(
        _mod_kernel,
        grid=(depth, nm // tn),
        in_specs=[
            pl.BlockSpec((SUBLANES, d), lambda l, j: (0, 0)),
            pl.BlockSpec((None, d, tn), lambda l, j: (l, 0, j)),
            pl.BlockSpec((None, 1, tn), lambda l, j: (l, 0, j)),
        ],
        out_specs=pl.BlockSpec((None, SUBLANES, tn), lambda l, j: (l, 0, j)),
        out_shape=jax.ShapeDtypeStruct((depth, SUBLANES, nm), F32),
        compiler_params=_cparams("parallel", "parallel"),
        name="mod_vectors",
    )(cc, w_mod, b_mod.reshape(depth, 1, nm))


def _dot_3pass(a, b):
    a_hi, a_lo = _split_bf16(a)
    b_hi, b_lo = _split_bf16(b)
    return _dot(a_hi, b_hi) + (_dot(a_hi, b_lo) + _dot(a_lo, b_hi))


def _fold_kernel(cc_ref, sc_ref, wf_ref, o_ref):
    wf = wf_ref[...]
    n = wf.shape[0]
    o_ref[:, :n] = _dot_3pass(cc_ref[...], wf)
    o_ref[:, n:] = _dot_3pass(sc_ref[...], wf)


def _fold_call(cc_bd, sc_bd, wf_bd):
    n = wf_bd.shape[0]
    return pl.pallas_call(
        _fold_kernel,
        out_shape=jax.ShapeDtypeStruct((n, 2 * n), F32),
        name="fourier_weight_fold",
    )(cc_bd, sc_bd, wf_bd)


def _inproj_kernel(*refs, add_pos, d_main):
    if add_pos:
        x_ref, prow_ref, pcol_ref, mod_ref, w_ref, wcs_ref, xo_ref, pm_ref, pq_ref = refs
    else:
        x_ref, mod_ref, w_ref, wcs_ref, pm_ref, pq_ref = refs
    tile, d = x_ref.shape
    m = mod_ref[...]
    sub = min(tile, INPROJ_ROWS)

    def normed(k):
        rows = pl.ds(k * sub, sub)
        if add_pos:
            half = d // 2
            nrow = sub // GRID_W
            prow = prow_ref[pl.ds(k * nrow, nrow), :]
            xr = x_ref[rows, :half].reshape(nrow, GRID_W, half) + prow[:, None, :]
            xq = x_ref[rows, half:].reshape(nrow, GRID_W, half) + pcol_ref[...][None, :, :]
            x = jnp.concatenate([xr.reshape(sub, half), xq.reshape(sub, half)], axis=-1)
            xo_ref[rows, :] = x
        else:
            x = x_ref[rows, :]
        return (_ln(x) * (1.0 + m[1:2, :]) + m[0:1, :]).astype(BF16)

    nsub = tile // sub
    h = normed(0)
    for k in range(nsub):
        rows = pl.ds(k * sub, sub)
        p = _dot(h, w_ref[...])
        if k + 1 < nsub:
            h = normed(k + 1)
        pm_ref[rows, :] = p[:, :d_main].astype(pm_ref.dtype)
        pq_ref[rows, :] = _dot(p[:, d_main:].astype(BF16), wcs_ref[...]).astype(pq_ref.dtype)


def _inproj_call(x, pos, mod, w_in, layer, wcs, tile):
    b, l, d = x.shape
    d_in = w_in.shape[2]
    d_c = wcs.shape[0]
    d_main = d_in - d_c
    add_pos = pos is not None
    tok = lambda bi, i: (bi, i, 0)
    in_specs = [pl.BlockSpec((None, tile, d), tok)]
    args = [x]
    if add_pos:
        prow, pcol = pos
        in_specs.append(pl.BlockSpec((tile // GRID_W, d // 2), lambda bi, i: (i, 0)))
        in_specs.append(pl.BlockSpec((GRID_W, d // 2), lambda bi, i: (0, 0)))
        args += [prow, pcol]
    in_specs += [
        pl.BlockSpec((None, N_MOD, d), lambda bi, i: (bi, 0, 0)),
        pl.BlockSpec((None, d, d_in), lambda bi, i: (layer, 0, 0)),
        pl.BlockSpec((d_c, 2 * d_c), lambda bi, i: (0, 0)),
    ]
    args += [mod, w_in, wcs]
    out_specs = [pl.BlockSpec((None, tile, d_main), tok), pl.BlockSpec((None, tile, 2 * d_c), tok)]
    out_shape = [jax.ShapeDtypeStruct((b, l, d_main), BF16), jax.ShapeDtypeStruct((b, l, 2 * d_c), BF16)]
    if add_pos:
        out_specs.insert(0, pl.BlockSpec((None, tile, d), tok))
        out_shape.insert(0, jax.ShapeDtypeStruct((b, l, d), F32))
    return pl.pallas_call(
        functools.partial(_inproj_kernel, add_pos=add_pos, d_main=d_main),
        grid=(b, l // tile),
        in_specs=in_specs,
        out_specs=out_specs,
        out_shape=out_shape,
        compiler_params=_cparams("parallel", "parallel"),
        name="in_proj",
    )(*args)


def _scan_scratch(tile, d_a):
    G = tile // SUBLANES
    pitch = G + SUBLANES
    return [
        pltpu.VMEM((d_a // LANES, SUBLANES * pitch, LANES), F32),
        pltpu.VMEM((G + 3, SUBLANES, d_a), F32),
        pltpu.VMEM((G, SUBLANES, d_a), F32),
        pltpu.VMEM((G, SUBLANES, d_a), F32),
        pltpu.VMEM((G, SUBLANES, d_a), F32),
        pltpu.VMEM((G, SUBLANES, d_a), F32),
        pltpu.VMEM((SUBLANES, d_a), F32),
    ]


def _scan_tile(x_ref, prev_ref, next_ref, cw_ref, cb_ref, wg_ref, ba_ref, bx_ref, lam_ref,
               out_ref, slab_ref, xt_ref, gr_ref, gi_ref, hl_ref, al_ref, carry_ref,
               *, reverse, seq_first, seq_last):
    tile, d_a = x_ref.shape
    G = tile // SUBLANES
    nq = d_a // LANES
    pitch = slab_ref.shape[1] // SUBLANES
    sub = lax.broadcasted_iota(jnp.int32, (SUBLANES, d_a), 0)
    bc = lambda row: jnp.broadcast_to(row, (SUBLANES, d_a))

    xv = x_ref[...].astype(F32)
    for s in range(SUBLANES):
        for q in range(nq):
            slab_ref[q, s * pitch:s * pitch + G, :] = xv[s * G:(s + 1) * G, q * LANES:(q + 1) * LANES]

    for g in range(G):
        for q in range(nq):
            xt_ref[g + 2, :, q * LANES:(q + 1) * LANES] = slab_ref[q, pl.ds(g, SUBLANES, stride=pitch), :]
    prev = jnp.where(seq_first, 0.0, prev_ref[...].astype(F32))
    nxt = jnp.where(seq_last, 0.0, next_ref[...].astype(F32))
    hrows = prev_ref.shape[0]
    xt_ref[0] = jnp.where(sub == 0, bc(prev[hrows - 2:hrows - 1, :]), pltpu.roll(xt_ref[G], 1, 0))
    xt_ref[1] = jnp.where(sub == 0, bc(prev[hrows - 1:hrows, :]), pltpu.roll(xt_ref[G + 1], 1, 0))
    xt_ref[G + 2] = jnp.where(sub == SUBLANES - 1, bc(nxt[0:1, :]), pltpu.roll(xt_ref[2], SUBLANES - 1, 0))
    cw = cw_ref[...]
    xh = (cb_ref[...][None] + cw[0:1, :][None] * xt_ref[0:G] + cw[1:2, :][None] * xt_ref[1:G + 1]
          + cw[2:3, :][None] * xt_ref[2:G + 2] + cw[3:4, :][None] * xt_ref[3:G + 3])
    xt_ref[0:G] = xh
    xhb = xh.reshape(tile, d_a).astype(BF16)

    pair = 2 * HEAD_DIM
    for j in range(d_a // pair):
        g2 = _dot(xhb[:, j * pair:(j + 1) * pair], wg_ref[j])
        gr_ref[:, :, j * pair:(j + 1) * pair] = g2[:, :pair].reshape(G, SUBLANES, pair)
        gi_ref[:, :, j * pair:(j + 1) * pair] = g2[:, pair:].reshape(G, SUBLANES, pair)
    lam = lam_ref[...]
    neg = -lam
    softplus = jnp.maximum(neg, 0.0) + jnp.log1p(jnp.exp(-jnp.abs(neg)))
    k2 = bc((-0.5 * RG_C / math.log(2.0)) * softplus)
    hba = bc(ba_ref[...])
    hbx = bc(bx_ref[...])

    h = jnp.zeros((SUBLANES, d_a), F32)
    acum = h + 1.0
    for t in range(G):
        g = (G - 1 - t) if reverse else t
        t_r = jnp.tanh(gr_ref[g] + hba)
        t_i = jnp.tanh(gi_ref[g] + hbx)
        l2 = k2 + k2 * t_r
        a = jnp.exp2(l2)
        w = jnp.tanh(l2 * (-math.log(2.0))) * (1.0 + a * a)
        xg = xt_ref[g]
        root = jnp.where(w > 0.0, w * lax.rsqrt(w), 0.0)
        h = a * h + root * (xg * t_i + xg)
        acum = acum * a
        hl_ref[g] = h
        al_ref[g] = acum

    first = SUBLANES - 1 if reverse else 0
    shift = SUBLANES - 1 if reverse else 1
    c_in = carry_ref[...]
    c = c_in
    for _ in range(SUBLANES - 1):
        c = jnp.where(sub == first, c_in, pltpu.roll(h + acum * c, shift, 0))
    last = SUBLANES - 1 - first
    carry_ref[...] = bc((h + acum * c)[last:last + 1, :])

    for g in range(G):
        hfull = hl_ref[g] + al_ref[g] * c
        for q in range(nq):
            slab_ref[q, pl.ds(g, SUBLANES, stride=pitch), :] = hfull[:, q * LANES:(q + 1) * LANES]
    for s in range(SUBLANES):
        for q in range(nq):
            out_ref[s * G:(s + 1) * G, q * LANES:(q + 1) * LANES] = (
                slab_ref[q, s * pitch:s * pitch + G, :].astype(out_ref.dtype))


def _lru_kernel(x_ref, prev_ref, next_ref, cw_ref, cb_ref, wg_ref, ba_ref, bx_ref, lam_ref, h0_ref,
                h_ref, hfin_ref, *scratch, reverse):
    i = pl.program_id(1)
    n = pl.num_programs(1)
    li = (n - 1 - i) if reverse else i

    @pl.when(i == 0)
    def _():
        scratch[-1][...] = h0_ref[...]

    _scan_tile(x_ref, prev_ref, next_ref, cw_ref, cb_ref, wg_ref, ba_ref, bx_ref, lam_ref,
               h_ref, *scratch, reverse=reverse, seq_first=li == 0, seq_last=li == n - 1)
    hfin_ref[...] = scratch[-1][...]


def _lru_call(p_main, conv_w_half, conv_b_half, wg, ba_half, bx_half, lam, h0, *, reverse, tile):
    b, l, _ = p_main.shape
    d_a = conv_w_half.shape[-1]
    n = l // tile
    per = tile // HALO
    nblk = l // HALO
    pos = (lambda i: n - 1 - i) if reverse else (lambda i: i)
    vec = lambda shape: pl.BlockSpec(shape, lambda bi, i: (0,) * len(shape))
    return pl.pallas_call(
        functools.partial(_lru_kernel, reverse=reverse),
        grid=(b, n),
        in_specs=[
            pl.BlockSpec((None, tile, d_a), lambda bi, i: (bi, pos(i), 0)),
            pl.BlockSpec((None, HALO, d_a), lambda bi, i: (bi, jnp.maximum(pos(i) * per - 1, 0), 0)),
            pl.BlockSpec((None, HALO, d_a), lambda bi, i: (bi, jnp.minimum((pos(i) + 1) * per, nblk - 1), 0)),
            vec((4, d_a)), vec((1, d_a)), vec(wg.shape), vec((1, d_a)), vec((1, d_a)), vec((1, d_a)),
            pl.BlockSpec((None, SUBLANES, d_a), lambda bi, i: (bi, 0, 0)),
        ],
        out_specs=[
            pl.BlockSpec((None, tile, d_a), lambda bi, i: (bi, pos(i), 0)),
            pl.BlockSpec((None, SUBLANES, d_a), lambda bi, i: (bi, 0, 0)),
        ],
        out_shape=[jax.ShapeDtypeStruct((b, l, d_a), BF16), jax.ShapeDtypeStruct((b, SUBLANES, d_a), F32)],
        scratch_shapes=_scan_scratch(tile, d_a),
        compiler_params=_cparams("arbitrary", "arbitrary"),
        name="lru_scan_bwd" if reverse else "lru_scan_fwd",
    )(p_main, p_main, p_main, conv_w_half, conv_b_half, wg, ba_half, bx_half, lam, h0)


def _dft_direct_kernel(pq_ref, m_ref, y_ref):
    pq = pq_ref[...]
    dc = pq.shape[-1] // 2
    rhs = jnp.concatenate([pq[:, :dc], pq[:, dc:]], axis=0).astype(BF16)
    y_ref[...] = _dot(m_ref[...], rhs).astype(y_ref.dtype)


def _dft_direct_call(pq, m):
    b, l, dc2 = pq.shape
    return pl.pallas_call(
        _dft_direct_kernel,
        grid=(b,),
        in_specs=[pl.BlockSpec((None, l, dc2), lambda bi: (bi, 0, 0)),
                  pl.BlockSpec((l, 2 * l), lambda bi: (0, 0))],
        out_specs=pl.BlockSpec((None, l, dc2 // 2), lambda bi: (bi, 0, 0)),
        out_shape=jax.ShapeDtypeStruct((b, l, dc2 // 2), BF16),
        compiler_params=_cparams("parallel"),
        name="dft_direct",
    )(pq, m)


def _slab_rows(a, width):
    return pl.ds(a * DFT_PITCH, width)


def _dft_stage1_kernel(x_ref, m_ref, ct_ref, st_ref, br_ref, bi_ref, xs_ref, os_ref):
    n1, tl, w = x_ref.shape
    dc = w // 2
    for a in range(n1):
        xa = x_ref[a].astype(F32)
        for q in range(w // LANES):
            xs_ref[q, _slab_rows(a, tl), :] = xa[:, q * LANES:(q + 1) * LANES]
    m = m_ref[...]
    for t in range(tl):
        every = pl.ds(t, n1, stride=DFT_PITCH)
        x = jnp.concatenate([xs_ref[q, every, :] for q in range(w // LANES)], axis=-1).astype(BF16)
        cs = _dot(m, x)
        ar = cs[:n1, :dc] - cs[n1:, dc:]
        ai = cs[:n1, dc:] + cs[n1:, :dc]
        ct = jnp.concatenate([ct_ref[t]] * (dc // LANES), axis=-1)
        st = jnp.concatenate([st_ref[t]] * (dc // LANES), axis=-1)
        br = ct * ar - st * ai
        bi = ct * ai + st * ar
        for q in range(dc // LANES):
            os_ref[0, q, every, :] = br[:, q * LANES:(q + 1) * LANES]
            os_ref[1, q, every, :] = bi[:, q * LANES:(q + 1) * LANES]
    for a in range(n1):
        for part, o_ref in enumerate((br_ref, bi_ref)):
            o_ref[a] = jnp.concatenate([os_ref[part, q, _slab_rows(a, tl), :] for q in range(dc // LANES)],
                                       axis=-1).astype(o_ref.dtype)


def _dft_stage1_call(x4, m1, ct, st):
    b, n1, n2, w = x4.shape
    dc = w // 2
    tl = min(DFT_TILE, n2)
    out_blk = pl.BlockSpec((None, n1, tl, dc), lambda j, bi: (bi, 0, j, 0))
    tab_blk = pl.BlockSpec((tl, n1, LANES), lambda j, bi: (j, 0, 0))
    return pl.pallas_call(
        _dft_stage1_kernel,
        grid=(n2 // tl, b),
        in_specs=[
            pl.BlockSpec((None, n1, tl, w), lambda j, bi: (bi, 0, j, 0)),
            pl.BlockSpec((2 * n1, n1), lambda j, bi: (0, 0)),
            tab_blk, tab_blk,
        ],
        out_specs=[out_blk, out_blk],
        out_shape=[jax.ShapeDtypeStruct((b, n1, n2, dc), BF16)] * 2,
        scratch_shapes=[pltpu.VMEM((w // LANES, n1 * DFT_PITCH, LANES), F32),
                        pltpu.VMEM((2, dc // LANES, n1 * DFT_PITCH, LANES), F32)],
        compiler_params=_cparams("parallel", "parallel"),
        name="dft_stage1",
    )(x4, m1, ct, st)


def _dft_stage2_kernel(br_ref, bi_ref, m_ref, y_ref, ys_ref):
    tk, n2, dc = br_ref.shape
    m = m_ref[...]
    for a in range(tk):
        y = _dot(m, jnp.concatenate([br_ref[a], bi_ref[a]], axis=0).astype(BF16))
        for q in range(dc // LANES):
            ys_ref[q, pl.ds(a, n2, stride=DFT_PITCH), :] = y[:, q * LANES:(q + 1) * LANES]
    for k2 in range(n2):
        y_ref[k2] = jnp.concatenate([ys_ref[q, _slab_rows(k2, tk), :] for q in range(dc // LANES)],
                                    axis=-1).astype(y_ref.dtype)


def _dft_stage2_call(br, bi, m2):
    b, n1, n2, dc = br.shape
    tk = min(DFT_TILE, n1)
    blk = pl.BlockSpec((None, tk, n2, dc), lambda bi_, k: (bi_, k, 0, 0))
    return pl.pallas_call(
        _dft_stage2_kernel,
        grid=(b, n1 // tk),
        in_specs=[blk, blk, pl.BlockSpec((n2, 2 * n2), lambda bi_, k: (0, 0))],
        out_specs=pl.BlockSpec((None, n2, tk, dc), lambda bi_, k: (bi_, 0, k, 0)),
        out_shape=jax.ShapeDtypeStruct((b, n2, n1, dc), BF16),
        scratch_shapes=[pltpu.VMEM((dc // LANES, n2 * DFT_PITCH, LANES), F32)],
        compiler_params=_cparams("parallel", "parallel"),
        name="dft_stage2",
    )(br, bi, m2)


def _dft_tables(l, scale):
    n2 = DFT_N2
    n1 = l // n2

    def cs(num, den):
        ang = (2.0 * np.pi / den) * (num % den).astype(np.float64)
        return np.cos(ang), np.sin(ang)

    k1 = np.arange(n1)
    c1, s1 = cs(np.outer(k1, k1), n1)
    l2 = np.arange(n2)
    ct, st = cs(np.outer(k1, l2), l)
    c2, s2 = cs(np.outer(l2, l2), n2)
    return c1, s1, ct, st, c2 * scale, s2 * scale


def _fourier_latent(pq, dc):
    b, l, _ = pq.shape
    n2 = DFT_N2
    n1 = l // n2
    c1, s1, ct, st, c2, s2 = _dft_tables(l, 1.0 / math.sqrt(l * HEAD_DIM))
    m1 = jnp.asarray(np.concatenate([c1, s1], axis=0), F32).astype(BF16)
    m2 = jnp.asarray(np.concatenate([c2, -s2], axis=1), F32).astype(BF16)
    lanes = lambda t: jnp.asarray(np.broadcast_to(t.T[:, :, None], (n2, n1, LANES)), F32)
    br, bi = _dft_stage1_call(pq.reshape(b, n1, n2, 2 * dc), m1, lanes(ct), lanes(st))
    y = _dft_stage2_call(br, bi, m2)
    return y.reshape(b, l, dc)


def _fourier_direct(pq, dc):
    b, l, _ = pq.shape
    k = np.arange(l)
    ang = (2.0 * np.pi / l) * (np.outer(k, k) % l).astype(np.float64)
    scale = 1.0 / math.sqrt(l * HEAD_DIM)
    m = jnp.asarray(np.concatenate([np.cos(ang) * scale, -np.sin(ang) * scale], axis=1), F32).astype(BF16)
    return _dft_direct_call(pq, m)


MIX_STAGES = 5


def _mix_rows(rows, out_ref, gate_ref, uv_ref, hf_ref, hb_ref, yc_ref, x_ref, mod_ref, seg_ref, ws_ref, sgb_ref,
              gmix_ref, wout_ref, g_ref, b_ref, *, alpha, d_a, d_b):
    seg = seg_ref[...]
    inv = 1.0 / HEAD_DIM

    def seg_mean(hi, lo):
        return (_dot(hi, seg) + _dot(lo, seg)) * inv

    def rms(t, g):
        return t * lax.rsqrt(jnp.mean(t * t, axis=-1, keepdims=True) + LN_EPS) * g

    gm = gmix_ref[...]
    z = _gelu(uv_ref[rows, :].astype(F32))
    u = z[:, :d_b]
    v = z[:, d_b:]
    v_hi, v_lo = _split_bf16(v)
    yield
    dv = v - seg_mean(v_hi, v_lo)
    q_hi, q_lo = _split_bf16(dv * dv)
    ya = rms(_gelu(gate_ref[rows, :].astype(F32)) * (hf_ref[rows, :].astype(F32) + hb_ref[rows, :].astype(F32)),
             gm[:, :d_a])
    yield
    vn = dv * lax.rsqrt(seg_mean(q_hi, q_lo) + LN_EPS)
    lane = lax.broadcasted_iota(jnp.int32, (CHUNK, d_b), 1)
    heads = d_b // HEAD_DIM
    rhs = jnp.concatenate(
        [jnp.where((lane >= hh * HEAD_DIM) & (lane < (hh + 1) * HEAD_DIM), vn, 0.0) for hh in range(heads)],
        axis=0).astype(BF16)
    yc = rms(yc_ref[rows, :].astype(F32), gm[:, d_a + d_b:])
    yield
    yb = rms(u * (_dot(ws_ref[...], rhs) + sgb_ref[...]), gm[:, d_a:d_a + d_b])
    y = jnp.concatenate([ya, yb, yc], axis=-1).astype(BF16)
    yield
    o = _dot(y, wout_ref[...])
    m = mod_ref[...]
    out_ref[rows, :] = _ln(alpha * x_ref[rows, :] + m[2:3, :] * o) * g_ref[...] + b_ref[...]
    yield


def _ffn_body(x, m, wup_ref, wdn_ref, g_ref, b_ref, res_ref, between, *, alpha):
    res_ref[...] = x
    h = (_ln(x) * (1.0 + m[4:5, :]) + m[3:4, :]).astype(BF16)
    d_ff = wdn_ref.shape[0]
    for j in range(d_ff // FF_CHUNK):
        cols = pl.ds(j * FF_CHUNK, FF_CHUNK)
        g = _dot(h, wup_ref[:, cols])
        up = _dot(h, wup_ref[:, pl.ds(d_ff + j * FF_CHUNK, FF_CHUNK)])
        act = (g * jax.nn.sigmoid(g) * up).astype(BF16)
        part = _dot(act, wdn_ref[cols, :])
        acc = part if j == 0 else acc + part
        between(j)
    return _ln(alpha * res_ref[...] + m[5:6, :] * acc) * g_ref[...] + b_ref[...]


def _mixffn_kernel(gate_ref, uv_ref, hf_ref, hb_ref, yc_ref, x_ref, modm_ref, modf_ref, seg_ref, ws_ref, sgb_ref,
                   gmix_ref, wout_ref, g1_ref, b1_ref, wup_ref, wdn_ref, g2_ref, b2_ref,
                   o_ref, x1_ref, res_ref, *, alpha, d_a, d_b):
    step = pl.program_id(0)
    nchunk = x_ref.shape[0] // CHUNK
    nff = wdn_ref.shape[0] // FF_CHUNK

    def mix_stages():
        todo = []
        for k0 in range(0, nchunk, 2):
            gens = [_mix_rows(pl.ds(c * CHUNK, CHUNK), x1_ref, gate_ref, uv_ref, hf_ref, hb_ref, yc_ref, x_ref,
                              modm_ref, seg_ref, ws_ref, sgb_ref, gmix_ref, wout_ref, g1_ref, b1_ref,
                              alpha=alpha, d_a=d_a, d_b=d_b) for c in range(k0, min(k0 + 2, nchunk))]
            todo += [g for _ in range(MIX_STAGES) for g in gens]
        return todo

    @pl.when(step == 0)
    def _():
        for gen in mix_stages():
            next(gen)

    @pl.when(step > 0)
    def _():
        todo = mix_stages()
        per_gap = -(-len(todo) // (nff - 1))

        def between(j):
            count = len(todo) if j == nff - 1 else per_gap
            for _ in range(min(count, len(todo))):
                next(todo.pop(0))

        o_ref[...] = _ffn_body(x1_ref[...], modf_ref[...], wup_ref, wdn_ref, g2_ref, b2_ref, res_ref, between,
                               alpha=alpha)


def _mixffn_call(p_main, hf, hb, yc, x, mod, seg, ws_cat, sgb, g_mix, w_out, g1, b1, w_up, w_down, g2, b2,
                 *, layer, alpha, tile):
    b, l, d = x.shape
    d_a = hf.shape[-1]
    d_c = yc.shape[-1]
    d_b = d_c
    n = l // tile
    total = b * n

    def mixed(j):
        def index(s):
            t = jnp.minimum(s, total - 1)
            return (t // n, t % n, j)
        return index

    def fed(s):
        t = jnp.maximum(s - 1, 0)
        return (t // n, t % n, 0)

    vec = lambda shape: pl.BlockSpec(shape, lambda s: (0,) * len(shape))
    res = lambda shape: pl.BlockSpec((None,) + shape, lambda s: (layer, 0, 0), pipeline_mode=pl.Buffered(1))
    return pl.pallas_call(
        functools.partial(_mixffn_kernel, alpha=alpha, d_a=d_a, d_b=d_b),
        grid=(total + 1,),
        in_specs=[
            pl.BlockSpec((None, tile, d_a), mixed(1)),
            pl.BlockSpec((None, tile, 2 * d_b), mixed(2)),
            pl.BlockSpec((None, tile, d_a), mixed(0)),
            pl.BlockSpec((None, tile, d_a), mixed(0)),
            pl.BlockSpec((None, tile, d_c), mixed(0)),
            pl.BlockSpec((None, tile, d), mixed(0)),
            pl.BlockSpec((None, N_MOD, d), lambda s: (jnp.minimum(s, total - 1) // n, 0, 0)),
            pl.BlockSpec((None, N_MOD, d), lambda s: (jnp.maximum(s - 1, 0) // n, 0, 0)),
            vec(seg.shape), vec(ws_cat.shape), vec(sgb.shape), vec((1, d)),
            pl.BlockSpec((None,) + w_out.shape[1:], lambda s: (layer, 0, 0)),
            vec((1, d)), vec((1, d)),
            res(w_up.shape[1:]), res(w_down.shape[1:]), vec((1, d)), vec((1, d)),
        ],
        out_specs=pl.BlockSpec((None, tile, d), fed),
        out_shape=jax.ShapeDtypeStruct((b, l, d), F32),
        scratch_shapes=[pltpu.VMEM((tile, d), F32)] * 2,
        compiler_params=_cparams("arbitrary"),
        name="mix_ffn",
    )(p_main, p_main, hf, hb, yc, x, mod, mod, seg, ws_cat, sgb, g_mix, w_out, g1, b1, w_up, w_down, g2, b2)


def _pos_tables(rows, dim):
    quarter = dim // 4
    freqs = POS_BASE ** (-jnp.arange(quarter, dtype=F32) / quarter)

    def enc(p):
        ang = p[:, None] * freqs[None, :]
        return jnp.concatenate([jnp.sin(ang), jnp.cos(ang)], -1)

    return enc(jnp.arange(rows, dtype=F32)), enc(jnp.arange(GRID_W, dtype=F32))


def _block_diag(blocks):
    n, r, c = blocks.shape
    eye = jnp.eye(n, dtype=blocks.dtype)
    return (eye[:, None, :, None] * blocks[:, :, None, :]).reshape(n * r, n * c)


def _gate_weights(wa, wx):
    heads = wa.shape[0]
    pa = jnp.stack([_block_diag(wa[2 * j:2 * j + 2]) for j in range(heads // 2)])
    px = jnp.stack([_block_diag(wx[2 * j:2 * j + 2]) for j in range(heads // 2)])
    return jnp.concatenate([pa, px], axis=-1).astype(BF16)


def kernel(x, c, ctx, c_ctx, w_mod, b_mod, w_in, conv_w, conv_b, lru_wa, lru_ba, lru_wx, lru_bx, lru_lam,
           sg_ws, sg_b, fourier_w, g_mix, w_out, ln1_g, ln1_b, w_up, w_down, ln2_g, ln2_b):
    bsz, seq, d = x.shape
    ctx_len = ctx.shape[1]
    depth = w_mod.shape[0]
    d_a = conv_w.shape[-1]
    d_c = fourier_w.shape[1] * fourier_w.shape[2]
    alpha = (2 * depth) ** 0.25
    lat_tile = min(512, seq)
    inproj_tile = min(2 * INPROJ_ROWS, seq)
    scan_tile = min(1024, seq)
    ctx_tile = min(256, ctx_len)

    cc = jnp.zeros((SUBLANES, d), F32).at[:bsz].set(c).at[bsz].set(c_ctx)
    mod = _mod_call(cc, w_mod, b_mod).reshape(depth, SUBLANES, N_MOD, d)

    pos = _pos_tables(seq // GRID_W, d)
    k = np.arange(HEAD_DIM)
    ang = (2.0 * np.pi / HEAD_DIM) * (np.outer(k, k) % HEAD_DIM).astype(np.float64)
    groups_c = d_c // HEAD_DIM
    cc_bd = jnp.asarray(np.kron(np.eye(groups_c), np.cos(ang)), F32)
    sc_bd = jnp.asarray(np.kron(np.eye(groups_c), np.sin(ang)), F32)
    seg = jnp.asarray(np.kron(np.eye(groups_c), np.ones((HEAD_DIM, HEAD_DIM))), BF16)
    zeros_state = jnp.zeros((bsz, SUBLANES, d_a), F32)
    w_in_b, w_out_b, w_up_b, w_dn_b = (w.astype(BF16) for w in (w_in, w_out, w_up, w_down))

    xl, xc = x, ctx
    for l in range(depth):
        last = l == depth - 1
        mod_l = mod[l, :bsz]
        mod_c = jnp.broadcast_to(mod[l, bsz][None], (bsz, N_MOD, d))
        wcs = _fold_call(cc_bd, sc_bd, _block_diag(fourier_w[l])).astype(BF16)
        cw, cb = 0.5 * conv_w[l], 0.5 * conv_b[l].reshape(1, d_a)
        lru = [(cw, cb, _gate_weights(lru_wa[l, dd], lru_wx[l, dd]), 0.5 * lru_ba[l, dd].reshape(1, d_a),
                0.5 * lru_bx[l, dd].reshape(1, d_a), lru_lam[l, dd].reshape(1, d_a)) for dd in range(2)]
        ws_cat = jnp.concatenate([sg_ws[l, hh] for hh in range(sg_ws.shape[1])], axis=1).astype(BF16)
        sgb = jnp.repeat(jnp.transpose(sg_b[l]), HEAD_DIM, axis=1)
        g1, b1 = ln1_g[l].reshape(1, d), ln1_b[l].reshape(1, d)
        g2, b2 = ln2_g[l].reshape(1, d), ln2_b[l].reshape(1, d)
        gm = g_mix[l].reshape(1, d)

        def mix_ffn(p_main, hf, hb, yc, xs, mods, tile):
            return _mixffn_call(p_main, hf, hb, yc, xs, mods, seg, ws_cat, sgb, gm, w_out_b, g1, b1,
                                w_up_b, w_dn_b, g2, b2, layer=l, alpha=alpha, tile=tile)

        pc_main, pc_pq = _inproj_call(xc, None, mod_c, w_in_b, l, wcs, ctx_tile)
        hf_c, sf_c = _lru_call(pc_main, *lru[0], zeros_state, reverse=False, tile=ctx_tile)
        hb_c, sb_c = _lru_call(pc_main, *lru[1], zeros_state, reverse=True, tile=ctx_tile)
        if not last:
            flat = lambda a: a.reshape(1, bsz * ctx_len, a.shape[-1])
            xc = mix_ffn(flat(pc_main), flat(hf_c), flat(hb_c), flat(_fourier_direct(pc_pq, d_c)), flat(xc),
                         mod_c[:1], math.gcd(bsz * ctx_len, lat_tile)).reshape(bsz, ctx_len, d)

        if l == 0:
            xl, pl_main, pl_pq = _inproj_call(xl, pos, mod_l, w_in_b, l, wcs, inproj_tile)
        else:
            pl_main, pl_pq = _inproj_call(xl, None, mod_l, w_in_b, l, wcs, min(2 * inproj_tile, seq))
        hf, _ = _lru_call(pl_main, *lru[0], sf_c, reverse=False, tile=scan_tile)
        hb, _ = _lru_call(pl_main, *lru[1], sb_c, reverse=True, tile=scan_tile)
        if seq > 2 * DFT_N2:
            yc = _fourier_latent(pl_pq, d_c)
        else:
            yc = _fourier_direct(pl_pq, d_c)
        xl = mix_ffn(pl_main, hf, hb, yc, xl, mod_l, lat_tile)
    return xl
```

```python
import functools
import math

import jax
import jax.numpy as jnp
import numpy as np
from jax import lax
from jax.experimental import pallas as pl
from jax.experimental.pallas import tpu as pltpu

HEAD_DIM = 64
CHUNK = 128
GRID_W = 64
RG_C = 8.0
LN_EPS = 1e-6
POS_BASE = 10000.0
N_MOD = 6
SUBLANES = 8
LANES = 128
HALO = 2 * SUBLANES
FF_CHUNK = 256
DFT_N2 = 128
DFT_TILE = 2 * HALO
DFT_PITCH = DFT_TILE + SUBLANES
INPROJ_ROWS = 512
VMEM_LIMIT = 56 * 1024 * 1024

F32 = jnp.float32
BF16 = jnp.bfloat16


def _dot(a, b):
    return jnp.dot(a, b, preferred_element_type=F32)


def _ln(x):
    mu = jnp.mean(x, axis=-1, keepdims=True)
    d = x - mu
    var = jnp.mean(d * d, axis=-1, keepdims=True)
    return d * lax.rsqrt(var + LN_EPS)


def _gelu(x):
    return 0.5 * x * (1.0 + lax.erf(x * (1.0 / math.sqrt(2.0))))


def _cparams(*sem):
    return pltpu.CompilerParams(dimension_semantics=sem, vmem_limit_bytes=VMEM_LIMIT)


def _split_bf16(t):
    hi = t.astype(BF16)
    return hi, (t - hi.astype(F32)).astype(BF16)


def _mod_kernel(c_ref, w_ref, b_ref, o_ref):
    c = c_ref[...]
    s = (c * jax.nn.sigmoid(c)).astype(BF16)
    o_ref[...] = _dot(s, w_ref[...].astype(BF16)) + b_ref[...]


def _mod_call(cc, w_mod, b_mod):
    depth, d, nm = w_mod.shape
    tn = 1536
    return pl.pallas_call(
        _mod_kernel,
        grid=(depth, nm // tn),
        in_specs=[
            pl.BlockSpec((SUBLANES, d), lambda l, j: (0, 0)),
            pl.BlockSpec((None, d, tn), lambda l, j: (l, 0, j)),
            pl.BlockSpec((None, 1, tn), lambda l, j: (l, 0, j)),
        ],
        out_specs=pl.BlockSpec((None, SUBLANES, tn), lambda l, j: (l, 0, j)),
        out_shape=jax.ShapeDtypeStruct((depth, SUBLANES, nm), F32),
        compiler_params=_cparams("parallel", "parallel"),
        name="mod_vectors",
    )(cc, w_mod, b_mod.reshape(depth, 1, nm))


def _dot_3pass(a, b):
    a_hi, a_lo = _split_bf16(a)
    b_hi, b_lo = _split_bf16(b)
    return _dot(a_hi, b_hi) + (_dot(a_hi, b_lo) + _dot(a_lo, b_hi))


def _fold_kernel(cc_ref, sc_ref, wf_ref, o_ref):
    wf = wf_ref[...]
    n = wf.shape[0]
    o_ref[:, :n] = _dot_3pass(cc_ref[...], wf)
    o_ref[:, n:] = _dot_3pass(sc_ref[...], wf)


def _fold_call(cc_bd, sc_bd, wf_bd):
    n = wf_bd.shape[0]
    return pl.pallas_call(
        _fold_kernel,
        out_shape=jax.ShapeDtypeStruct((n, 2 * n), F32),
        name="fourier_weight_fold",
    )(cc_bd, sc_bd, wf_bd)


def _inproj_kernel(*refs, add_pos, d_main):
    if add_pos:
        x_ref, prow_ref, pcol_ref, mod_ref, w_ref, wcs_ref, xo_ref, pm_ref, pq_ref = refs
    else:
        x_ref, mod_ref, w_ref, wcs_ref, pm_ref, pq_ref = refs
    tile, d = x_ref.shape
    m = mod_ref[...]
    sub = min(tile, INPROJ_ROWS)

    def normed(k):
        rows = pl.ds(k * sub, sub)
        if add_pos:
            half = d // 2
            nrow = sub // GRID_W
            prow = prow_ref[pl.ds(k * nrow, nrow), :]
            xr = x_ref[rows, :half].reshape(nrow, GRID_W, half) + prow[:, None, :]
            xq = x_ref[rows, half:].reshape(nrow, GRID_W, half) + pcol_ref[...][None, :, :]
            x = jnp.concatenate([xr.reshape(sub, half), xq.reshape(sub, half)], axis=-1)
            xo_ref[rows, :] = x
        else:
            x = x_ref[rows, :]
        return (_ln(x) * (1.0 + m[1:2, :]) + m[0:1, :]).astype(BF16)

    nsub = tile // sub
    h = normed(0)
    for k in range(nsub):
        rows = pl.ds(k * sub, sub)
        p = _dot(h, w_ref[...])
        if k + 1 < nsub:
            h = normed(k + 1)
        pm_ref[rows, :] = p[:, :d_main].astype(pm_ref.dtype)
        pq_ref[rows, :] = _dot(p[:, d_main:].astype(BF16), wcs_ref[...]).astype(pq_ref.dtype)


def _inproj_call(x, pos, mod, w_in, layer, wcs, tile):
    b, l, d = x.shape
    d_in = w_in.shape[2]
    d_c = wcs.shape[0]
    d_main = d_in - d_c
    add_pos = pos is not None
    tok = lambda bi, i: (bi, i, 0)
    in_specs = [pl.BlockSpec((None, tile, d), tok)]
    args = [x]
    if add_pos:
        prow, pcol = pos
        in_specs.append(pl.BlockSpec((tile // GRID_W, d // 2), lambda bi, i: (i, 0)))
        in_specs.append(pl.BlockSpec((GRID_W, d // 2), lambda bi, i: (0, 0)))
        args += [prow, pcol]
    in_specs += [
        pl.BlockSpec((None, N_MOD, d), lambda bi, i: (bi, 0, 0)),
        pl.BlockSpec((None, d, d_in), lambda bi, i: (layer, 0, 0), pipeline_mode=pl.Buffered(1)),
        pl.BlockSpec((d_c, 2 * d_c), lambda bi, i: (0, 0), pipeline_mode=pl.Buffered(1)),
    ]
    args += [mod, w_in, wcs]
    out_specs = [pl.BlockSpec((None, tile, d_main), tok), pl.BlockSpec((None, tile, 2 * d_c), tok)]
    out_shape = [jax.ShapeDtypeStruct((b, l, d_main), BF16), jax.ShapeDtypeStruct((b, l, 2 * d_c), BF16)]
    if add_pos:
        out_specs.insert(0, pl.BlockSpec((None, tile, d), tok))
        out_shape.insert(0, jax.ShapeDtypeStruct((b, l, d), F32))
    return pl.pallas_call(
        functools.partial(_inproj_kernel, add_pos=add_pos, d_main=d_main),
        grid=(b, l // tile),
        in_specs=in_specs,
        out_specs=out_specs,
        out_shape=out_shape,
        compiler_params=_cparams("parallel", "parallel"),
        name="in_proj",
    )(*args)


def _scan_scratch(tile, d_a):
    G = tile // SUBLANES
    pitch = G + SUBLANES
    return [
        pltpu.VMEM((d_a // LANES, SUBLANES * pitch, LANES), F32),
        pltpu.VMEM((G + 3, SUBLANES, d_a), F32),
        pltpu.VMEM((G, SUBLANES, d_a), F32),
        pltpu.VMEM((G, SUBLANES, d_a), F32),
        pltpu.VMEM((G, SUBLANES, d_a), F32),
        pltpu.VMEM((G, SUBLANES, d_a), F32),
        pltpu.VMEM((SUBLANES, d_a), F32),
    ]


def _scan_tile(x_ref, prev_ref, next_ref, cw_ref, cb_ref, wg_ref, ba_ref, bx_ref, lam_ref,
               out_ref, slab_ref, xt_ref, gr_ref, gi_ref, hl_ref, al_ref, carry_ref,
               *, reverse, seq_first, seq_last):
    tile, d_a = x_ref.shape
    G = tile // SUBLANES
    nq = d_a // LANES
    pitch = slab_ref.shape[1] // SUBLANES
    sub = lax.broadcasted_iota(jnp.int32, (SUBLANES, d_a), 0)
    bc = lambda row: jnp.broadcast_to(row, (SUBLANES, d_a))

    xv = x_ref[...].astype(F32)
    for s in range(SUBLANES):
        for q in range(nq):
            slab_ref[q, s * pitch:s * pitch + G, :] = xv[s * G:(s + 1) * G, q * LANES:(q + 1) * LANES]

    for g in range(G):
        for q in range(nq):
            xt_ref[g + 2, :, q * LANES:(q + 1) * LANES] = slab_ref[q, pl.ds(g, SUBLANES, stride=pitch), :]
    prev = jnp.where(seq_first, 0.0, prev_ref[...].astype(F32))
    nxt = jnp.where(seq_last, 0.0, next_ref[...].astype(F32))
    hrows = prev_ref.shape[0]
    xt_ref[0] = jnp.where(sub == 0, bc(prev[hrows - 2:hrows - 1, :]), pltpu.roll(xt_ref[G], 1, 0))
    xt_ref[1] = jnp.where(sub == 0, bc(prev[hrows - 1:hrows, :]), pltpu.roll(xt_ref[G + 1], 1, 0))
    xt_ref[G + 2] = jnp.where(sub == SUBLANES - 1, bc(nxt[0:1, :]), pltpu.roll(xt_ref[2], SUBLANES - 1, 0))
    cw = cw_ref[...]
    xh = (cb_ref[...][None] + cw[0:1, :][None] * xt_ref[0:G] + cw[1:2, :][None] * xt_ref[1:G + 1]
          + cw[2:3, :][None] * xt_ref[2:G + 2] + cw[3:4, :][None] * xt_ref[3:G + 3])
    xt_ref[0:G] = xh
    xhb = xh.reshape(tile, d_a).astype(BF16)

    pair = 2 * HEAD_DIM
    for j in range(d_a // pair):
        g2 = _dot(xhb[:, j * pair:(j + 1) * pair], wg_ref[j])
        gr_ref[:, :, j * pair:(j + 1) * pair] = g2[:, :pair].reshape(G, SUBLANES, pair)
        gi_ref[:, :, j * pair:(j + 1) * pair] = g2[:, pair:].reshape(G, SUBLANES, pair)
    lam = lam_ref[...]
    neg = -lam
    softplus = jnp.maximum(neg, 0.0) + jnp.log1p(jnp.exp(-jnp.abs(neg)))
    k2 = bc((-0.5 * RG_C / math.log(2.0)) * softplus)
    hba = bc(ba_ref[...])
    hbx = bc(bx_ref[...])

    h = jnp.zeros((SUBLANES, d_a), F32)
    acum = h + 1.0
    for t in range(G):
        g = (G - 1 - t) if reverse else t
        t_r = jnp.tanh(gr_ref[g] + hba)
        t_i = jnp.tanh(gi_ref[g] + hbx)
        l2 = k2 + k2 * t_r
        a = jnp.exp2(l2)
        w = jnp.tanh(l2 * (-math.log(2.0))) * (1.0 + a * a)
        xg = xt_ref[g]
        root = jnp.where(w > 0.0, w * lax.rsqrt(w), 0.0)
        h = a * h + root * (xg * t_i + xg)
        acum = acum * a
        hl_ref[g] = h
        al_ref[g] = acum

    first = SUBLANES - 1 if reverse else 0
    shift = SUBLANES - 1 if reverse else 1
    c_in = carry_ref[...]
    c = c_in
    for _ in range(SUBLANES - 1):
        c = jnp.where(sub == first, c_in, pltpu.roll(h + acum * c, shift, 0))
    last = SUBLANES - 1 - first
    carry_ref[...] = bc((h + acum * c)[last:last + 1, :])

    for g in range(G):
        hfull = hl_ref[g] + al_ref[g] * c
        for q in range(nq):
            slab_ref[q, pl.ds(g, SUBLANES, stride=pitch), :] = hfull[:, q * LANES:(q + 1) * LANES]
    for s in range(SUBLANES):
        for q in range(nq):
            out_ref[s * G:(s + 1) * G, q * LANES:(q + 1) * LANES] = (
                slab_ref[q, s * pitch:s * pitch + G, :].astype(out_ref.dtype))


def _lru_kernel(x_ref, prev_ref, next_ref, cw_ref, cb_ref, wg_ref, ba_ref, bx_ref, lam_ref, h0_ref,
                h_ref, hfin_ref, *scratch, reverse):
    i = pl.program_id(1)
    n = pl.num_programs(1)
    li = (n - 1 - i) if reverse else i

    @pl.when(i == 0)
    def _():
        scratch[-1][...] = h0_ref[...]

    _scan_tile(x_ref, prev_ref, next_ref, cw_ref, cb_ref, wg_ref, ba_ref, bx_ref, lam_ref,
               h_ref, *scratch, reverse=reverse, seq_first=li == 0, seq_last=li == n - 1)
    hfin_ref[...] = scratch[-1][...]


def _lru_call(p_main, conv_w_half, conv_b_half, wg, ba_half, bx_half, lam, h0, *, reverse, tile):
    b, l, _ = p_main.shape
    d_a = conv_w_half.shape[-1]
    n = l // tile
    per = tile // HALO
    nblk = l // HALO
    pos = (lambda i: n - 1 - i) if reverse else (lambda i: i)
    vec = lambda shape: pl.BlockSpec(shape, lambda bi, i: (0,) * len(shape))
    return pl.pallas_call(
        functools.partial(_lru_kernel, reverse=reverse),
        grid=(b, n),
        in_specs=[
            pl.BlockSpec((None, tile, d_a), lambda bi, i: (bi, pos(i), 0)),
            pl.BlockSpec((None, HALO, d_a), lambda bi, i: (bi, jnp.maximum(pos(i) * per - 1, 0), 0)),
            pl.BlockSpec((None, HALO, d_a), lambda bi, i: (bi, jnp.minimum((pos(i) + 1) * per, nblk - 1), 0)),
            vec((4, d_a)), vec((1, d_a)), vec(wg.shape), vec((1, d_a)), vec((1, d_a)), vec((1, d_a)),
            pl.BlockSpec((None, SUBLANES, d_a), lambda bi, i: (bi, 0, 0)),
        ],
        out_specs=[
            pl.BlockSpec((None, tile, d_a), lambda bi, i: (bi, pos(i), 0)),
            pl.BlockSpec((None, SUBLANES, d_a), lambda bi, i: (bi, 0, 0)),
        ],
        out_shape=[jax.ShapeDtypeStruct((b, l, d_a), BF16), jax.ShapeDtypeStruct((b, SUBLANES, d_a), F32)],
        scratch_shapes=_scan_scratch(tile, d_a),
        compiler_params=_cparams("arbitrary", "arbitrary"),
        name="lru_scan_bwd" if reverse else "lru_scan_fwd",
    )(p_main, p_main, p_main, conv_w_half, conv_b_half, wg, ba_half, bx_half, lam, h0)


def _dft_direct_kernel(pq_ref, m_ref, y_ref):
    pq = pq_ref[...]
    dc = pq.shape[-1] // 2
    rhs = jnp.concatenate([pq[:, :dc], pq[:, dc:]], axis=0).astype(BF16)
    y_ref[...] = _dot(m_ref[...], rhs).astype(y_ref.dtype)


def _dft_direct_call(pq, m):
    b, l, dc2 = pq.shape
    return pl.pallas_call(
        _dft_direct_kernel,
        grid=(b,),
        in_specs=[pl.BlockSpec((None, l, dc2), lambda bi: (bi, 0, 0)),
                  pl.BlockSpec((l, 2 * l), lambda bi: (0, 0))],
        out_specs=pl.BlockSpec((None, l, dc2 // 2), lambda bi: (bi, 0, 0)),
        out_shape=jax.ShapeDtypeStruct((b, l, dc2 // 2), BF16),
        compiler_params=_cparams("parallel"),
        name="dft_direct",
    )(pq, m)


def _slab_rows(a, width):
    return pl.ds(a * DFT_PITCH, width)


def _dft_stage1_kernel(x_ref, m_ref, ct_ref, st_ref, br_ref, bi_ref, xs_ref, os_ref):
    n1, tl, w = x_ref.shape
    dc = w // 2
    for a in range(n1):
        xa = x_ref[a].astype(F32)
        for q in range(w // LANES):
            xs_ref[q, _slab_rows(a, tl), :] = xa[:, q * LANES:(q + 1) * LANES]
    m = m_ref[...]
    for t in range(tl):
        every = pl.ds(t, n1, stride=DFT_PITCH)
        x = jnp.concatenate([xs_ref[q, every, :] for q in range(w // LANES)], axis=-1).astype(BF16)
        cs = _dot(m, x)
        ar = cs[:n1, :dc] - cs[n1:, dc:]
        ai = cs[:n1, dc:] + cs[n1:, :dc]
        ct = jnp.concatenate([ct_ref[t]] * (dc // LANES), axis=-1)
        st = jnp.concatenate([st_ref[t]] * (dc // LANES), axis=-1)
        br = ct * ar - st * ai
        bi = ct * ai + st * ar
        for q in range(dc // LANES):
            os_ref[0, q, every, :] = br[:, q * LANES:(q + 1) * LANES]
            os_ref[1, q, every, :] = bi[:, q * LANES:(q + 1) * LANES]
    for a in range(n1):
        for part, o_ref in enumerate((br_ref, bi_ref)):
            o_ref[a] = jnp.concatenate([os_ref[part, q, _slab_rows(a, tl), :] for q in range(dc // LANES)],
                                       axis=-1).astype(o_ref.dtype)


def _dft_stage1_call(x4, m1, ct, st):
    b, n1, n2, w = x4.shape
    dc = w // 2
    tl = min(DFT_TILE, n2)
    out_blk = pl.BlockSpec((None, n1, tl, dc), lambda j, bi: (bi, 0, j, 0))
    tab_blk = pl.BlockSpec((tl, n1, LANES), lambda j, bi: (j, 0, 0))
    return pl.pallas_call(
        _dft_stage1_kernel,
        grid=(n2 // tl, b),
        in_specs=[
            pl.BlockSpec((None, n1, tl, w), lambda j, bi: (bi, 0, j, 0)),
            pl.BlockSpec((2 * n1, n1), lambda j, bi: (0, 0)),
            tab_blk, tab_blk,
        ],
        out_specs=[out_blk, out_blk],
        out_shape=[jax.ShapeDtypeStruct((b, n1, n2, dc), BF16)] * 2,
        scratch_shapes=[pltpu.VMEM((w // LANES, n1 * DFT_PITCH, LANES), F32),
                        pltpu.VMEM((2, dc // LANES, n1 * DFT_PITCH, LANES), F32)],
        compiler_params=_cparams("parallel", "parallel"),
        name="dft_stage1",
    )(x4, m1, ct, st)


def _dft_stage2_kernel(br_ref, bi_ref, m_ref, y_ref, ys_ref):
    tk, n2, dc = br_ref.shape
    m = m_ref[...]
    for a in range(tk):
        y = _dot(m, jnp.concatenate([br_ref[a], bi_ref[a]], axis=0).astype(BF16))
        for q in range(dc // LANES):
            ys_ref[q, pl.ds(a, n2, stride=DFT_PITCH), :] = y[:, q * LANES:(q + 1) * LANES]
    for k2 in range(n2):
        y_ref[k2] = jnp.concatenate([ys_ref[q, _slab_rows(k2, tk), :] for q in range(dc // LANES)],
                                    axis=-1).astype(y_ref.dtype)


def _dft_stage2_call(br, bi, m2):
    b, n1, n2, dc = br.shape
    tk = min(DFT_TILE, n1)
    blk = pl.BlockSpec((None, tk, n2, dc), lambda bi_, k: (bi_, k, 0, 0))
    return pl.pallas_call(
        _dft_stage2_kernel,
        grid=(b, n1 // tk),
        in_specs=[blk, blk, pl.BlockSpec((n2, 2 * n2), lambda bi_, k: (0, 0))],
        out_specs=pl.BlockSpec((None, n2, tk, dc), lambda bi_, k: (bi_, 0, k, 0)),
        out_shape=jax.ShapeDtypeStruct((b, n2, n1, dc), BF16),
        scratch_shapes=[pltpu.VMEM((dc // LANES, n2 * DFT_PITCH, LANES), F32)],
        compiler_params=_cparams("parallel", "parallel"),
        name="dft_stage2",
    )(br, bi, m2)


def _dft_tables(l, scale):
    n2 = DFT_N2
    n1 = l // n2

    def cs(num, den):
        ang = (2.0 * np.pi / den) * (num % den).astype(np.float64)
        return np.cos(ang), np.sin(ang)

    k1 = np.arange(n1)
    c1, s1 = cs(np.outer(k1, k1), n1)
    l2 = np.arange(n2)
    ct, st = cs(np.outer(k1, l2), l)
    c2, s2 = cs(np.outer(l2, l2), n2)
    return c1, s1, ct, st, c2 * scale, s2 * scale


def _fourier_latent(pq, dc):
    b, l, _ = pq.shape
    n2 = DFT_N2
    n1 = l // n2
    c1, s1, ct, st, c2, s2 = _dft_tables(l, 1.0 / math.sqrt(l * HEAD_DIM))
    m1 = jnp.asarray(np.concatenate([c1, s1], axis=0), F32).astype(BF16)
    m2 = jnp.asarray(np.concatenate([c2, -s2], axis=1), F32).astype(BF16)
    lanes = lambda t: jnp.asarray(np.broadcast_to(t.T[:, :, None], (n2, n1, LANES)), F32)
    br, bi = _dft_stage1_call(pq.reshape(b, n1, n2, 2 * dc), m1, lanes(ct), lanes(st))
    y = _dft_stage2_call(br, bi, m2)
    return y.reshape(b, l, dc)


def _fourier_direct(pq, dc):
    b, l, _ = pq.shape
    k = np.arange(l)
    ang = (2.0 * np.pi / l) * (np.outer(k, k) % l).astype(np.float64)
    scale = 1.0 / math.sqrt(l * HEAD_DIM)
    m = jnp.asarray(np.concatenate([np.cos(ang) * scale, -np.sin(ang) * scale], axis=1), F32).astype(BF16)
    return _dft_direct_call(pq, m)


MIX_STAGES = 5


def _mix_rows(rows, out_ref, gate_ref, uv_ref, hf_ref, hb_ref, yc_ref, x_ref, mod_ref, seg_ref, ws_ref, sgb_ref,
              gmix_ref, wout_ref, g_ref, b_ref, *, alpha, d_a, d_b):
    seg = seg_ref[...]
    inv = 1.0 / HEAD_DIM

    def seg_mean(hi, lo):
        return (_dot(hi, seg) + _dot(lo, seg)) * inv

    def rms(t, g):
        return t * lax.rsqrt(jnp.mean(t * t, axis=-1, keepdims=True) + LN_EPS) * g

    gm = gmix_ref[...]
    z = _gelu(uv_ref[rows, :].astype(F32))
    u = z[:, :d_b]
    v = z[:, d_b:]
    v_hi, v_lo = _split_bf16(v)
    yield
    dv = v - seg_mean(v_hi, v_lo)
    q_hi, q_lo = _split_bf16(dv * dv)
    ya = rms(_gelu(gate_ref[rows, :].astype(F32)) * (hf_ref[rows, :].astype(F32) + hb_ref[rows, :].astype(F32)),
             gm[:, :d_a])
    yield
    vn = dv * lax.rsqrt(seg_mean(q_hi, q_lo) + LN_EPS)
    lane = lax.broadcasted_iota(jnp.int32, (CHUNK, d_b), 1)
    heads = d_b // HEAD_DIM
    rhs = jnp.concatenate(
        [jnp.where((lane >= hh * HEAD_DIM) & (lane < (hh + 1) * HEAD_DIM), vn, 0.0) for hh in range(heads)],
        axis=0).astype(BF16)
    yc = rms(yc_ref[rows, :].astype(F32), gm[:, d_a + d_b:])
    yield
    yb = rms(u * (_dot(ws_ref[...], rhs) + sgb_ref[...]), gm[:, d_a:d_a + d_b])
    y = jnp.concatenate([ya, yb, yc], axis=-1).astype(BF16)
    yield
    o = _dot(y, wout_ref[...])
    m = mod_ref[...]
    out_ref[rows, :] = _ln(alpha * x_ref[rows, :] + m[2:3, :] * o) * g_ref[...] + b_ref[...]
    yield


def _ffn_body(x, m, wup_ref, wdn_ref, g_ref, b_ref, res_ref, between, *, alpha):
    res_ref[...] = x
    h = (_ln(x) * (1.0 + m[4:5, :]) + m[3:4, :]).astype(BF16)
    d_ff = wdn_ref.shape[0]
    for j in range(d_ff // FF_CHUNK):
        cols = pl.ds(j * FF_CHUNK, FF_CHUNK)
        g = _dot(h, wup_ref[:, cols])
        up = _dot(h, wup_ref[:, pl.ds(d_ff + j * FF_CHUNK, FF_CHUNK)])
        act = (g * jax.nn.sigmoid(g) * up).astype(BF16)
        part = _dot(act, wdn_ref[cols, :])
        acc = part if j == 0 else acc + part
        between(j)
    return _ln(alpha * res_ref[...] + m[5:6, :] * acc) * g_ref[...] + b_ref[...]


def _mixffn_kernel(gate_ref, uv_ref, hf_ref, hb_ref, yc_ref, x_ref, modm_ref, modf_ref, seg_ref, ws_ref, sgb_ref,
                   gmix_ref, wout_ref, g1_ref, b1_ref, wup_ref, wdn_ref, g2_ref, b2_ref,
                   o_ref, x1_ref, res_ref, *, alpha, d_a, d_b):
    step = pl.program_id(0)
    nchunk = x_ref.shape[0] // CHUNK
    nff = wdn_ref.shape[0] // FF_CHUNK

    def mix_stages():
        todo = []
        for k0 in range(0, nchunk, 2):
            gens = [_mix_rows(pl.ds(c * CHUNK, CHUNK), x1_ref, gate_ref, uv_ref, hf_ref, hb_ref, yc_ref, x_ref,
                              modm_ref, seg_ref, ws_ref, sgb_ref, gmix_ref, wout_ref, g1_ref, b1_ref,
                              alpha=alpha, d_a=d_a, d_b=d_b) for c in range(k0, min(k0 + 2, nchunk))]
            todo += [g for _ in range(MIX_STAGES) for g in gens]
        return todo

    @pl.when(step == 0)
    def _():
        for gen in mix_stages():
            next(gen)

    @pl.when(step > 0)
    def _():
        todo = mix_stages()
        per_gap = -(-len(todo) // (nff - 1))

        def between(j):
            count = len(todo) if j == nff - 1 else per_gap
            for _ in range(min(count, len(todo))):
                next(todo.pop(0))

        o_ref[...] = _ffn_body(x1_ref[...], modf_ref[...], wup_ref, wdn_ref, g2_ref, b2_ref, res_ref, between,
                               alpha=alpha)


def _mixffn_call(p_main, hf, hb, yc, x, mod, seg, ws_cat, sgb, g_mix, w_out, g1, b1, w_up, w_down, g2, b2,
                 *, layer, alpha, tile):
    b, l, d = x.shape
    d_a = hf.shape[-1]
    d_c = yc.shape[-1]
    d_b = d_c
    n = l // tile
    total = b * n

    def mixed(j):
        def index(s):
            t = jnp.minimum(s, total - 1)
            return (t // n, t % n, j)
        return index

    def fed(s):
        t = jnp.maximum(s - 1, 0)
        return (t // n, t % n, 0)

    vec = lambda shape: pl.BlockSpec(shape, lambda s: (0,) * len(shape))
    res = lambda shape: pl.BlockSpec((None,) + shape, lambda s: (layer, 0, 0), pipeline_mode=pl.Buffered(1))
    return pl.pallas_call(
        functools.partial(_mixffn_kernel, alpha=alpha, d_a=d_a, d_b=d_b),
        grid=(total + 1,),
        in_specs=[
            pl.BlockSpec((None, tile, d_a), mixed(1)),
            pl.BlockSpec((None, tile, 2 * d_b), mixed(2)),
            pl.BlockSpec((None, tile, d_a), mixed(0)),
            pl.BlockSpec((None, tile, d_a), mixed(0)),
            pl.BlockSpec((None, tile, d_c), mixed(0)),
            pl.BlockSpec((None, tile, d), mixed(0)),
            pl.BlockSpec((None, N_MOD, d), lambda s: (jnp.minimum(s, total - 1) // n, 0, 0)),
            pl.BlockSpec((None, N_MOD, d), lambda s: (jnp.maximum(s - 1, 0) // n, 0, 0)),
            vec(seg.shape), vec(ws_cat.shape), vec(sgb.shape), vec((1, d)),
            pl.BlockSpec((None,) + w_out.shape[1:], lambda s: (layer, 0, 0)),
            vec((1, d)), vec((1, d)),
            res(w_up.shape[1:]), res(w_down.shape[1:]), vec((1, d)), vec((1, d)),
        ],
        out_specs=pl.BlockSpec((None, tile, d), fed),
        out_shape=jax.ShapeDtypeStruct((b, l, d), F32),
        scratch_shapes=[pltpu.VMEM((tile, d), F32)] * 2,
        compiler_params=_cparams("arbitrary"),
        name="mix_ffn",
    )(p_main, p_main, hf, hb, yc, x, mod, mod, seg, ws_cat, sgb, g_mix, w_out, g1, b1, w_up, w_down, g2, b2)


def _pos_tables(rows, dim):
    quarter = dim // 4
    freqs = POS_BASE ** (-jnp.arange(quarter, dtype=F32) / quarter)

    def enc(p):
        ang = p[:, None] * freqs[None, :]
        return jnp.concatenate([jnp.sin(ang), jnp.cos(ang)], -1)

    return enc(jnp.arange(rows, dtype=F32)), enc(jnp.arange(GRID_W, dtype=F32))


def _block_diag(blocks):
    n, r, c = blocks.shape
    eye = jnp.eye(n, dtype=blocks.dtype)
    return (eye[:, None, :, None] * blocks[:, :, None, :]).reshape(n * r, n * c)


def _gate_weights(wa, wx):
    heads = wa.shape[0]
    pa = jnp.stack([_block_diag(wa[2 * j:2 * j + 2]) for j in range(heads // 2)])
    px = jnp.stack([_block_diag(wx[2 * j:2 * j + 2]) for j in range(heads // 2)])
    return jnp.concatenate([pa, px], axis=-1).astype(BF16)


def kernel(x, c, ctx, c_ctx, w_mod, b_mod, w_in, conv_w, conv_b, lru_wa, lru_ba, lru_wx, lru_bx, lru_lam,
           sg_ws, sg_b, fourier_w, g_mix, w_out, ln1_g, ln1_b, w_up, w_down, ln2_g, ln2_b):
    bsz, seq, d = x.shape
    ctx_len = ctx.shape[1]
    depth = w_mod.shape[0]
    d_a = conv_w.shape[-1]
    d_c = fourier_w.shape[1] * fourier_w.shape[2]
    alpha = (2 * depth) ** 0.25
    lat_tile = min(512, seq)
    inproj_tile = min(4 * INPROJ_ROWS, seq)
    scan_tile = min(1024, seq)
    ctx_tile = min(256, ctx_len)

    cc = jnp.zeros((SUBLANES, d), F32).at[:bsz].set(c).at[bsz].set(c_ctx)
    mod = _mod_call(cc, w_mod, b_mod).reshape(depth, SUBLANES, N_MOD, d)

    pos = _pos_tables(seq // GRID_W, d)
    k = np.arange(HEAD_DIM)
    ang = (2.0 * np.pi / HEAD_DIM) * (np.outer(k, k) % HEAD_DIM).astype(np.float64)
    groups_c = d_c // HEAD_DIM
    cc_bd = jnp.asarray(np.kron(np.eye(groups_c), np.cos(ang)), F32)
    sc_bd = jnp.asarray(np.kron(np.eye(groups_c), np.sin(ang)), F32)
    seg = jnp.asarray(np.kron(np.eye(groups_c), np.ones((HEAD_DIM, HEAD_DIM))), BF16)
    zeros_state = jnp.zeros((bsz, SUBLANES, d_a), F32)
    w_in_b, w_out_b, w_up_b, w_dn_b = (w.astype(BF16) for w in (w_in, w_out, w_up, w_down))

    xl, xc = x, ctx
    for l in range(depth):
        last = l == depth - 1
        mod_l = mod[l, :bsz]
        mod_c = jnp.broadcast_to(mod[l, bsz][None], (bsz, N_MOD, d))
        wcs = _fold_call(cc_bd, sc_bd, _block_diag(fourier_w[l])).astype(BF16)
        cw, cb = 0.5 * conv_w[l], 0.5 * conv_b[l].reshape(1, d_a)
        lru = [(cw, cb, _gate_weights(lru_wa[l, dd], lru_wx[l, dd]), 0.5 * lru_ba[l, dd].reshape(1, d_a),
                0.5 * lru_bx[l, dd].reshape(1, d_a), lru_lam[l, dd].reshape(1, d_a)) for dd in range(2)]
        ws_cat = jnp.concatenate([sg_ws[l, hh] for hh in range(sg_ws.shape[1])], axis=1).astype(BF16)
        sgb = jnp.repeat(jnp.transpose(sg_b[l]), HEAD_DIM, axis=1)
        g1, b1 = ln1_g[l].reshape(1, d), ln1_b[l].reshape(1, d)
        g2, b2 = ln2_g[l].reshape(1, d), ln2_b[l].reshape(1, d)
        gm = g_mix[l].reshape(1, d)

        def mix_ffn(p_main, hf, hb, yc, xs, mods, tile):
            return _mixffn_call(p_main, hf, hb, yc, xs, mods, seg, ws_cat, sgb, gm, w_out_b, g1, b1,
                                w_up_b, w_dn_b, g2, b2, layer=l, alpha=alpha, tile=tile)

        pc_main, pc_pq = _inproj_call(xc, None, mod_c, w_in_b, l, wcs, ctx_tile)
        hf_c, sf_c = _lru_call(pc_main, *lru[0], zeros_state, reverse=False, tile=ctx_tile)
        hb_c, sb_c = _lru_call(pc_main, *lru[1], zeros_state, reverse=True, tile=ctx_tile)
        if not last:
            flat = lambda a: a.reshape(1, bsz * ctx_len, a.shape[-1])
            xc = mix_ffn(flat(pc_main), flat(hf_c), flat(hb_c), flat(_fourier_direct(pc_pq, d_c)), flat(xc),
                         mod_c[:1], math.gcd(bsz * ctx_len, lat_tile)).reshape(bsz, ctx_len, d)

        if l == 0:
            xl, pl_main, pl_pq = _inproj_call(xl, pos, mod_l, w_in_b, l, wcs, inproj_tile)
        else:
            pl_main, pl_pq = _inproj_call(xl, None, mod_l, w_in_b, l, wcs, inproj_tile)
        hf, _ = _lru_call(pl_main, *lru[0], sf_c, reverse=False, tile=scan_tile)
        hb, _ = _lru_call(pl_main, *lru[1], sb_c, reverse=True, tile=scan_tile)
        if seq > 2 * DFT_N2:
            yc = _fourier_latent(pl_pq, d_c)
        else:
            yc = _fourier_direct(pl_pq, d_c)
        xl = mix_ffn(pl_main, hf, hb, yc, xl, mod_l, lat_tile)
    return xl
```
